```python
import math
import jax, jax.numpy as jnp
from jax import lax
import numpy as np

D_MODEL = 1024
BATCH = 8
SEQ = 4096
DEPTH = 4

HGRN_HEADS = 8
HGRN_DK = 128
HGRN_DV = D_MODEL // HGRN_HEADS
D_HGRN_K = HGRN_HEADS * HGRN_DK
D_HGRN_V = HGRN_HEADS * HGRN_DV
MLSTM_HEADS = 4
D_MLSTM = D_MODEL
MLSTM_DH = D_MLSTM // MLSTM_HEADS
MLSTM_CONV = 4
QKV_BLOCK = 4
N_QKV_BLOCKS = D_MLSTM // QKV_BLOCK
CHUNK = 64
IN_SIZES = (D_HGRN_K, D_HGRN_K, D_HGRN_V, D_HGRN_V, D_MLSTM, D_MLSTM, D_MODEL, D_MODEL)
D_IN = sum(IN_SIZES)
IN_SPLITS = tuple(int(s) for s in np.cumsum(IN_SIZES)[:-1])
D_FF = 2816
N_EXPERTS = 8
TOP_K = 2
D_FF_EXPERT = 3584
N_DENSE = (DEPTH + 1) // 2
N_MOE = DEPTH // 2
EPS = 1e-6
NEG = -1e30

kernel_name = 'hybrid_hgrn2_mlstm_moe_adaln'


def rmsnorm(x, g):
    xf = x.astype(jnp.float32)
    y = xf * lax.rsqrt(jnp.mean(xf * xf, axis=-1, keepdims=True) + EPS)
    return (y * g).astype(x.dtype)


def head_rmsnorm(o):
    return o * lax.rsqrt(jnp.mean(o * o, axis=-1, keepdims=True) + EPS)


def head_layernorm(o):
    mu = jnp.mean(o, axis=-1, keepdims=True)
    oc = o - mu
    return oc * lax.rsqrt(jnp.mean(oc * oc, axis=-1, keepdims=True) + EPS)


def to_chunks(t, heads):
    b, s, w = t.shape
    return t.reshape(b, s // CHUNK, CHUNK, heads, w // heads).transpose(1, 0, 3, 2, 4)


def gate_chunks(t):
    b, s, h = t.shape
    return t.reshape(b, s // CHUNK, CHUNK, h).transpose(1, 0, 3, 2)


def from_chunks(o):
    nc, b, h, l, d = o.shape
    return o.transpose(1, 0, 3, 2, 4).reshape(b, nc * l, h, d)


def hgrn_lower_bounds(lb_raw):
    p = jax.nn.softmax(lb_raw.astype(jnp.float32), axis=0)
    return jnp.cumsum(p, axis=0) - p[0:1]


def hgrn2_chunked(q, k, v, log_f):
    L = q.shape[-2]
    causal = jnp.tril(jnp.ones((L, L), dtype=bool))

    def step(state, inp):
        qc, kc, vc, lfc = inp
        b = jnp.cumsum(lfc, axis=-2)
        diff = b[..., :, None, :] - b[..., None, :, :]
        decay = jnp.exp(jnp.where(causal[:, :, None], diff, NEG))
        scores = jnp.einsum('bhtd,bhtsd,bhsd->bhts', qc, decay, kc)
        o = (jnp.einsum('bhts,bhsv->bhtv', scores, vc)
             + jnp.einsum('bhtd,bhdv->bhtv', qc * jnp.exp(b), state))
        b_last = b[..., -1:, :]
        state = (jnp.exp(b_last[..., 0, :])[..., None] * state
                 + jnp.einsum('bhsd,bhsv->bhdv', kc * jnp.exp(b_last - b), vc))
        return state, o

    nc, bsz, heads, l, dk = q.shape
    init = jnp.zeros((bsz, heads, dk, v.shape[-1]), jnp.float32)
    _, o = lax.scan(step, init, (q, k, v, log_f))
    return o


def hgrn2_branch(q, f_pre, i, g, lb, gnorm):
    f32 = jnp.float32
    bsz, s, _ = q.shape
    q = jax.nn.silu(q.astype(f32))
    fp = f_pre.astype(f32)
    lb = lb.astype(f32)
    f = lb + (1.0 - lb) * jax.nn.sigmoid(fp)
    log_f = jnp.log(f)
    k = (1.0 - lb) * jax.nn.sigmoid(-fp)
    o = hgrn2_chunked(to_chunks(q, HGRN_HEADS), to_chunks(k, HGRN_HEADS),
                      to_chunks(i.astype(f32), HGRN_HEADS), to_chunks(log_f, HGRN_HEADS))
    o = head_rmsnorm(from_chunks(o)).reshape(bsz, s, D_HGRN_V)
    return o * gnorm * jax.nn.silu(g.astype(f32))


def causal_conv(x, w, b):
    kw = w.shape[0]
    y = lax.conv_general_dilated(x, w[:, None, :], window_strides=(1,), padding=[(kw - 1, 0)],
                                 dimension_numbers=('NWC', 'WIO', 'NWC'),
                                 feature_group_count=x.shape[-1])
    return y + b


def blockdiag(x, w):
    bsz, s, _ = x.shape
    xb = x.reshape(bsz, s, N_QKV_BLOCKS, QKV_BLOCK)
    return jnp.einsum('bsni,nio->bsno', xb, w).reshape(bsz, s, D_MLSTM)


def mlstm_chunked(q, k, v, log_i, log_f):
    L = q.shape[-2]
    causal = jnp.tril(jnp.ones((L, L), dtype=bool))

    def step(carry, inp):
        C, n, m = carry
        qc, kc, vc, lic, lfc = inp
        b = jnp.cumsum(lfc, axis=-1)
        d = jnp.where(causal, b[..., :, None] - b[..., None, :] + lic[..., None, :], NEG)
        m_inter = b + m[..., None]
        m_t = jnp.maximum(m_inter, jnp.max(d, axis=-1))
        s = jnp.einsum('bhtd,bhsd->bhts', qc, kc) * jnp.exp(d - m_t[..., None])
        w_inter = jnp.exp(m_inter - m_t)
        num = (jnp.einsum('bhts,bhsv->bhtv', s, vc)
               + w_inter[..., None] * jnp.einsum('bhtd,bhdv->bhtv', qc, C))
        den = jnp.sum(s, axis=-1) + w_inter * jnp.einsum('bhtd,bhd->bht', qc, n)
        h = num / jnp.maximum(jnp.abs(den), jnp.exp(-m_t))[..., None]
        b_last = b[..., -1]
        g = b_last[..., None] - b + lic
        m_new = jnp.maximum(b_last + m, jnp.max(g, axis=-1))
        w_s = jnp.exp(g - m_new[..., None])
        decay = jnp.exp(b_last + m - m_new)
        C = decay[..., None, None] * C + jnp.einsum('bhs,bhsd,bhsv->bhdv', w_s, kc, vc)
        n = decay[..., None] * n + jnp.einsum('bhs,bhsd->bhd', w_s, kc)
        return (C, n, m_new), h

    nc, bsz, heads, l, dk = q.shape
    init = (jnp.zeros((bsz, heads, dk, v.shape[-1]), jnp.float32),
            jnp.zeros((bsz, heads, dk), jnp.float32),
            jnp.zeros((bsz, heads), jnp.float32))
    _, h = lax.scan(step, init, (q, k, v, log_i, log_f))
    return h


def mlstm_branch(xm, o_pre, conv_w, conv_b, wq, wk, wv, w_ig, b_ig, w_fg, b_fg, gnorm, skip):
    f32 = jnp.float32
    bsz, s, _ = xm.shape
    xm = xm.astype(f32)
    xc = jax.nn.silu(causal_conv(xm, conv_w.astype(f32), conv_b.astype(f32)))
    q = blockdiag(xc, wq)
    k = blockdiag(xc, wk)
    v = blockdiag(xm, wv)
    qkv = jnp.concatenate([q, k, v], axis=-1)
    log_i = (qkv @ w_ig + b_ig).astype(f32)
    log_f = jax.nn.log_sigmoid((qkv @ w_fg + b_fg).astype(f32))
    hc = mlstm_chunked(to_chunks(q, MLSTM_HEADS), to_chunks(k * (MLSTM_DH ** -0.5), MLSTM_HEADS),
                       to_chunks(v, MLSTM_HEADS), gate_chunks(log_i), gate_chunks(log_f))
    h = head_layernorm(from_chunks(hc)).reshape(bsz, s, D_MLSTM) * gnorm + skip * xc
    return jax.nn.sigmoid(o_pre.astype(f32)) * h


def hybrid_mixer(h, lb, w_in, hgrn_gnorm, conv_w, conv_b, wq, wk, wv, w_ig, b_ig, w_fg, b_fg,
                 mlstm_gnorm, skip, w_proj_a, w_proj_b, w_out):
    proj = h @ w_in
    q_a, f_a, i_a, g_a, xm_b, o_b, gate_a, gate_b = jnp.split(proj, IN_SPLITS, axis=-1)
    y_a = hgrn2_branch(q_a, f_a, i_a, g_a, lb, hgrn_gnorm) @ w_proj_a
    y_b = mlstm_branch(xm_b, o_b, conv_w, conv_b, wq, wk, wv, w_ig, b_ig, w_fg, b_fg,
                       mlstm_gnorm, skip) @ w_proj_b
    mixed = (jax.nn.sigmoid(gate_a.astype(jnp.float32)) * y_a
             + jax.nn.sigmoid(gate_b.astype(jnp.float32)) * y_b)
    return (mixed @ w_out).astype(h.dtype)


def swiglu(h, w1, w3, w2):
    return (jax.nn.silu(h @ w1) * (h @ w3)) @ w2


def moe_swiglu(h, router, w1, w3, w2):
    logits = (h @ router).astype(jnp.float32)
    top_v, top_i = lax.top_k(logits, TOP_K)
    top_w = jax.nn.softmax(top_v, axis=-1)
    gates = jnp.sum(jax.nn.one_hot(top_i, N_EXPERTS, dtype=jnp.float32) * top_w[..., None], axis=-2)
    out = jnp.zeros(h.shape, jnp.float32)
    for e in range(N_EXPERTS):
        out = out + gates[..., e:e + 1] * swiglu(h, w1[e], w3[e], w2[e])
    return out.astype(h.dtype)


def setup_inputs(seed: int = 0) -> dict:
    key = jax.random.key(seed)
    ks = iter(jax.random.split(key, 40))
    f32 = jnp.float32

    def nrm(shape, scale):
        return scale * jax.random.normal(next(ks), shape, f32)

    D = D_MODEL
    return {
        'x': nrm((BATCH, SEQ, D), 1.0),
        'c': nrm((BATCH, D), 1.0),
        'w_ada': nrm((DEPTH, D, 6 * D), 0.3 * D ** -0.5),
        'b_ada': nrm((DEPTH, 6 * D), 0.02),
        'g_pre_mix': 1.0 + nrm((DEPTH, D), 0.05),
        'g_post_mix': 1.0 + nrm((DEPTH, D), 0.05),
        'g_pre_ffn': 1.0 + nrm((DEPTH, D), 0.05),
        'g_post_ffn': 1.0 + nrm((DEPTH, D), 0.05),
        'w_in': nrm((DEPTH, D, D_IN), D ** -0.5),
        'hgrn_lb': 1.0 + nrm((DEPTH, D_HGRN_K), 0.5),
        'hgrn_gnorm': 1.0 + nrm((DEPTH, D_HGRN_V), 0.05),
        'mlstm_conv_w': nrm((DEPTH, MLSTM_CONV, D_MLSTM), MLSTM_CONV ** -0.5),
        'mlstm_conv_b': nrm((DEPTH, D_MLSTM), 0.02),
        'mlstm_wq': nrm((DEPTH, N_QKV_BLOCKS, QKV_BLOCK, QKV_BLOCK), QKV_BLOCK ** -0.5),
        'mlstm_wk': nrm((DEPTH, N_QKV_BLOCKS, QKV_BLOCK, QKV_BLOCK), QKV_BLOCK ** -0.5),
        'mlstm_wv': nrm((DEPTH, N_QKV_BLOCKS, QKV_BLOCK, QKV_BLOCK), QKV_BLOCK ** -0.5),
        'mlstm_w_ig': nrm((DEPTH, 3 * D_MLSTM, MLSTM_HEADS), 0.1 * (3 * D_MLSTM) ** -0.5),
        'mlstm_b_ig': nrm((DEPTH, MLSTM_HEADS), 0.1),
        'mlstm_w_fg': nrm((DEPTH, 3 * D_MLSTM, MLSTM_HEADS), 0.1 * (3 * D_MLSTM) ** -0.5),
        'mlstm_b_fg': jnp.linspace(3.0, 6.0, MLSTM_HEADS, dtype=f32)[None, :] + nrm((DEPTH, MLSTM_HEADS), 0.1),
        'mlstm_gnorm': 1.0 + nrm((DEPTH, D_MLSTM), 0.05),
        'mlstm_skip': 1.0 + nrm((DEPTH, D_MLSTM), 0.05),
        'w_proj_a': nrm((DEPTH, D_HGRN_V, D), D_HGRN_V ** -0.5),
        'w_proj_b': nrm((DEPTH, D_MLSTM, D), D_MLSTM ** -0.5),
        'w_out': nrm((DEPTH, D, D), D ** -0.5),
        'ffn_w1': nrm((N_DENSE, D, D_FF), D ** -0.5),
        'ffn_w3': nrm((N_DENSE, D, D_FF), D ** -0.5),
        'ffn_w2': nrm((N_DENSE, D_FF, D), D_FF ** -0.5),
        'moe_router': nrm((N_MOE, D, N_EXPERTS), D ** -0.5),
        'moe_w1': nrm((N_MOE, N_EXPERTS, D, D_FF_EXPERT), D ** -0.5),
        'moe_w3': nrm((N_MOE, N_EXPERTS, D, D_FF_EXPERT), D ** -0.5),
        'moe_w2': nrm((N_MOE, N_EXPERTS, D_FF_EXPERT, D), D_FF_EXPERT ** -0.5),
    }


def reference(x, c, w_ada, b_ada, g_pre_mix, g_post_mix, g_pre_ffn, g_post_ffn, w_in, hgrn_lb,
              hgrn_gnorm, mlstm_conv_w, mlstm_conv_b, mlstm_wq, mlstm_wk, mlstm_wv, mlstm_w_ig,
              mlstm_b_ig, mlstm_w_fg, mlstm_b_fg, mlstm_gnorm, mlstm_skip, w_proj_a, w_proj_b, w_out,
              ffn_w1, ffn_w3, ffn_w2, moe_router, moe_w1, moe_w3, moe_w2):
    lower_bounds = hgrn_lower_bounds(hgrn_lb)
    c_act = jax.nn.silu(c)
    for l in range(DEPTH):
        mod = c_act @ w_ada[l] + b_ada[l]
        sh_m, sc_m, gt_m, sh_f, sc_f, gt_f = [m[:, None, :] for m in jnp.split(mod, 6, axis=-1)]
        h = rmsnorm(x, g_pre_mix[l]) * (1.0 + sc_m) + sh_m
        y = hybrid_mixer(h, lower_bounds[l], w_in[l], hgrn_gnorm[l], mlstm_conv_w[l], mlstm_conv_b[l],
                         mlstm_wq[l], mlstm_wk[l], mlstm_wv[l], mlstm_w_ig[l], mlstm_b_ig[l],
                         mlstm_w_fg[l], mlstm_b_fg[l], mlstm_gnorm[l], mlstm_skip[l],
                         w_proj_a[l], w_proj_b[l], w_out[l])
        x = x + (gt_m * rmsnorm(y, g_post_mix[l])).astype(x.dtype)
        h = rmsnorm(x, g_pre_ffn[l]) * (1.0 + sc_f) + sh_f
        j = l // 2
        if l % 2 == 0:
            y = swiglu(h, ffn_w1[j], ffn_w3[j], ffn_w2[j]).astype(x.dtype)
        else:
            y = moe_swiglu(h, moe_router[j], moe_w1[j], moe_w3[j], moe_w2[j])
        x = x + (gt_f * rmsnorm(y, g_post_ffn[l])).astype(x.dtype)
    return x
```

```python
import functools

import jax
import jax.numpy as jnp
from jax import lax
from jax.experimental import pallas as pl
from jax.experimental.pallas import tpu as pltpu

F32 = jnp.float32
BF16 = jnp.bfloat16
EPS = 1e-6
NEG = -1e30
LANES = 128

D_MODEL = 1024
HGRN_HEADS = 8
HGRN_DH = 128
MLSTM_HEADS = 4
MLSTM_DH = 256
MLSTM_CONV = 4
QKV_BLOCK = 4
N_EXPERTS = 8
N_SEG = 8

VMEM_LIMIT = 56 * 1024 * 1024


def _cparams(sem):
    return pltpu.CompilerParams(dimension_semantics=sem, vmem_limit_bytes=VMEM_LIMIT)


def _sigmoid(x):
    return 1.0 / (1.0 + jnp.exp(-x))


def _dot(a, b):
    return jnp.dot(a, b, preferred_element_type=F32)


def _dot_nt(a, b):
    return lax.dot_general(a, b, (((1,), (1,)), ((), ())), preferred_element_type=F32)


def _dot_tn(a, b):
    return lax.dot_general(a, b, (((0,), (0,)), ((), ())), preferred_element_type=F32)


def _split3(x):
    hi = x.astype(BF16)
    r1 = x - hi.astype(F32)
    mid = r1.astype(BF16)
    lo = (r1 - mid.astype(F32)).astype(BF16)
    return hi, mid, lo


def _chunk_tril(n, chunk):
    r = lax.broadcasted_iota(jnp.int32, (n, n), 0)
    c = lax.broadcasted_iota(jnp.int32, (n, n), 1)
    keep = (r >= c) & ((r // chunk) == (c // chunk))
    return jnp.where(keep, 1.0, 0.0).astype(BF16)


def _cumsum_rows(tri, x):
    hi, mid, lo = _split3(x)
    return _dot(tri, hi) + _dot(tri, mid) + _dot(tri, lo)


def _rms_mod(x, g, sc, sh):
    ms = jnp.mean(x * x, axis=-1, keepdims=True)
    return (x * lax.rsqrt(ms + EPS) * g) * (1.0 + sc) + sh


def _ada_kernel(c_ref, w_ref, b_ref, o_ref):
    c = c_ref[...]
    ca = (c * _sigmoid(c)).astype(BF16)
    o_ref[0] = _dot(ca, w_ref[0].astype(BF16)) + b_ref[0]


def _ada(c_pad, w_ada, b_ada):
    depth, d, six_d = w_ada.shape
    bp = c_pad.shape[0]
    tn = 1024
    return pl.pallas_call(
        _ada_kernel,
        grid=(depth, six_d // tn),
        in_specs=[
            pl.BlockSpec((bp, d), lambda l, j: (0, 0)),
            pl.BlockSpec((1, d, tn), lambda l, j: (l, 0, j)),
            pl.BlockSpec((1, 1, tn), lambda l, j: (l, 0, j)),
        ],
        out_specs=pl.BlockSpec((1, bp, tn), lambda l, j: (l, 0, j)),
        out_shape=jax.ShapeDtypeStruct((depth, bp, six_d), F32),
        compiler_params=_cparams(("parallel", "parallel")),
        name="ada_mod",
    )(c_pad, w_ada, b_ada.reshape(depth, 1, six_d))


def _in_kernel(x_ref, g_ref, sc_ref, sh_ref, lb_ref, w_ref, p_ref, lf_ref, h_scr):
    j = pl.program_id(1)

    @pl.when(j == 0)
    def _():
        h_scr[...] = _rms_mod(x_ref[...], g_ref[...], sc_ref[0], sh_ref[0]).astype(BF16)

    acc = _dot(h_scr[...], w_ref[...])

    @pl.when((j == 0) | (j == 3))
    def _():
        p_ref[...] = (acc * _sigmoid(acc)).astype(BF16)

    @pl.when(j == 1)
    def _():
        lb = lb_ref[...]
        sg = _sigmoid(acc)
        lf_ref[...] = jnp.log(lb + (1.0 - lb) * sg)
        p_ref[...] = ((1.0 - lb) * (1.0 - sg)).astype(BF16)

    @pl.when((j == 2) | (j == 4))
    def _():
        p_ref[...] = acc.astype(BF16)

    @pl.when(j >= 5)
    def _():
        p_ref[...] = _sigmoid(acc).astype(BF16)


def _in_proj(x2, g_pre, mod3, lb, w_in, *, seq, tile):
    n, d = x2.shape
    tiles_per_row = seq // tile
    return pl.pallas_call(
        _in_kernel,
        grid=(n // tile, N_SEG),
        in_specs=[
            pl.BlockSpec((tile, d), lambda i, j: (i, 0)),
            pl.BlockSpec((1, d), lambda i, j: (0, 0)),
            pl.BlockSpec((1, 1, d), lambda i, j: (i // tiles_per_row, 0, 1)),
            pl.BlockSpec((1, 1, d), lambda i, j: (i // tiles_per_row, 0, 0)),
            pl.BlockSpec((1, d), lambda i, j: (0, 0)),
            pl.BlockSpec((d, d), lambda i, j: (0, j)),
        ],
        out_specs=[
            pl.BlockSpec((tile, d), lambda i, j: (i, j)),
            pl.BlockSpec((tile, d), lambda i, j: (i, 0)),
        ],
        out_shape=[
            jax.ShapeDtypeStruct((n, N_SEG * d), BF16),
            jax.ShapeDtypeStruct((n, d), F32),
        ],
        scratch_shapes=[pltpu.VMEM((tile, d), BF16)],
        compiler_params=_cparams(("parallel", "arbitrary")),
        name="in_proj",
    )(x2, g_pre, mod3, mod3, lb, w_in)


def _hgrn_kernel(q_ref, k_ref, v_ref, sg_ref, lf_ref, gn_ref, o_ref, st_scr, *, tb, chunk):
    @pl.when(pl.program_id(1) == 0)
    def _():
        st_scr[...] = jnp.zeros_like(st_scr)

    tri = _chunk_tril(tb, chunk)
    b_all = _cumsum_rows(tri, lf_ref[...])
    r_io = lax.broadcasted_iota(jnp.int32, (chunk, chunk), 0)
    c_io = lax.broadcasted_iota(jnp.int32, (chunk, chunk), 1)
    causal = r_io >= c_io
    mid = chunk // 2 - 1
    for ci in range(tb // chunk):
        r0 = ci * chunk
        b = b_all[r0:r0 + chunk]
        b_mid = b[mid:mid + 1]
        b_last = b[chunk - 1:chunk]
        q = q_ref[r0:r0 + chunk, :].astype(F32)
        k = k_ref[r0:r0 + chunk, :].astype(F32)
        v = v_ref[r0:r0 + chunk, :]
        q_in = (q * jnp.exp(b - b_mid)).astype(BF16)
        k_in = (k * jnp.exp(b_mid - b)).astype(BF16)
        q_st = (q * jnp.exp(b)).astype(BF16)
        k_st = (k * jnp.exp(b_last - b)).astype(BF16)
        dec = jnp.exp(b_last)
        for h in range(HGRN_HEADS):
            cs = slice(h * HGRN_DH, (h + 1) * HGRN_DH)
            sc = jnp.where(causal, _dot_nt(q_in[:, cs], k_in[:, cs]), 0.0).astype(BF16)
            st = st_scr[h]
            o = _dot(sc, v[:, cs]) + _dot_nt(q_st[:, cs], st.astype(BF16))
            st_scr[h] = st * dec[:, cs] + _dot_tn(v[:, cs], k_st[:, cs])
            ms = jnp.mean(o * o, axis=-1, keepdims=True)
            y = o * lax.rsqrt(ms + EPS) * gn_ref[:, cs] * sg_ref[r0:r0 + chunk, cs].astype(F32)
            o_ref[r0:r0 + chunk, cs] = y.astype(BF16)


def _hgrn(p, lf, gnorm, *, batch, seq, tb, chunk):
    n, d = lf.shape
    nt = seq // tb
    col = lambda c: pl.BlockSpec((tb, d), lambda b, t: (b * nt + t, c))
    return pl.pallas_call(
        functools.partial(_hgrn_kernel, tb=tb, chunk=chunk),
        grid=(batch, nt),
        in_specs=[col(0), col(1), col(2), col(3), col(0),
                  pl.BlockSpec((1, d), lambda b, t: (0, 0))],
        out_specs=col(0),
        out_shape=jax.ShapeDtypeStruct((n, d), BF16),
        scratch_shapes=[pltpu.VMEM((HGRN_HEADS, HGRN_DH, HGRN_DH), F32)],
        compiler_params=_cparams(("parallel", "arbitrary")),
        name="hgrn2",
    )(p, p, p, p, lf, gnorm)


def _mlstm_kernel(xm_ref, so_ref, cw_ref, cb_ref, bdq_ref, bdk_ref, bdv_ref, wg_ref, bg_ref,
                  gn_ref, sk_ref, o_ref, xbuf, c_scr, n_scr, m_scr, *, L):
    d = D_MODEL

    @pl.when(pl.program_id(1) == 0)
    def _():
        xbuf[0:8, :] = jnp.zeros((8, d), F32)
        c_scr[...] = jnp.zeros_like(c_scr)
        n_scr[...] = jnp.zeros_like(n_scr)
        m_scr[...] = jnp.zeros_like(m_scr)

    xmb = xm_ref[...]
    xm = xmb.astype(F32)
    xbuf[8:8 + L, :] = xm
    cw = cw_ref[...]
    conv = (cb_ref[...] + cw[3:4] * xm + cw[2:3] * xbuf[7:7 + L, :]
            + cw[1:2] * xbuf[6:6 + L, :] + cw[0:1] * xbuf[5:5 + L, :])
    xbuf[0:8, :] = xm[L - 8:L]
    xc = conv * _sigmoid(conv)
    xcb = xc.astype(BF16)

    qs, ks, vs = [], [], []
    for g in range(d // LANES):
        gs = slice(g * LANES, (g + 1) * LANES)
        qs.append(_dot(xcb[:, gs], bdq_ref[g]))
        ks.append(_dot(xcb[:, gs], bdk_ref[g]))
        vs.append(_dot(xmb[:, gs], bdv_ref[g]))
    q = jnp.concatenate(qs, axis=-1)
    k = jnp.concatenate(ks, axis=-1)
    v = jnp.concatenate(vs, axis=-1)
    qb, kb, vb = q.astype(BF16), k.astype(BF16), v.astype(BF16)

    gates = (_dot(qb, wg_ref[0:d, :]) + _dot(kb, wg_ref[d:2 * d, :])
             + _dot(vb, wg_ref[2 * d:3 * d, :]) + bg_ref[...])
    lf = jnp.minimum(gates, 0.0) - jnp.log(1.0 + jnp.exp(-jnp.abs(gates)))
    bcs = _cumsum_rows(_chunk_tril(L, L), lf)
    gates_t = gates.T
    bcs_t = bcs.T

    r_io = lax.broadcasted_iota(jnp.int32, (L, L), 0)
    c_io = lax.broadcasted_iota(jnp.int32, (L, L), 1)
    causal = r_io >= c_io
    kscale = MLSTM_DH ** -0.5
    for h in range(MLSTM_HEADS):
        cs = slice(h * MLSTM_DH, (h + 1) * MLSTM_DH)
        a_row = gates_t[h:h + 1, :] - bcs_t[4 + h:5 + h, :]
        b_col = bcs[:, 4 + h:5 + h]
        a_col = gates[:, h:h + 1] - b_col
        m_prev = m_scr[h][:, 0:1]
        am = jnp.where(causal, a_row, NEG)
        m_run = jnp.maximum(jnp.max(am, axis=-1, keepdims=True), m_prev)
        e = jnp.exp(am - m_run)
        qh = q[:, cs]
        khs = k[:, cs] * kscale
        s = _dot_nt(qb[:, cs], khs.astype(BF16)) * e
        w_inter = jnp.exp(m_prev - m_run)
        num = _dot(s.astype(BF16), vb[:, cs]) + w_inter * _dot(qb[:, cs], c_scr[h].astype(BF16))
        den = (jnp.sum(s, axis=-1, keepdims=True)
               + w_inter * jnp.sum(qh * n_scr[h], axis=-1, keepdims=True))
        m_t = b_col + m_run
        hh = num / jnp.maximum(jnp.abs(den), jnp.exp(-m_t))
        m_last = m_run[L - 1:L]
        w_s = jnp.exp(a_col - m_last)
        dec = jnp.exp(m_prev - m_last)
        kw = khs * w_s
        c_scr[h] = dec * c_scr[h] + _dot_tn(kw.astype(BF16), vb[:, cs])
        n_scr[h] = dec * n_scr[h] + jnp.sum(kw, axis=0, keepdims=True)
        m_scr[h] = jnp.broadcast_to(b_col[L - 1:L] + m_last, (1, LANES))
        mu = jnp.mean(hh, axis=-1, keepdims=True)
        hc = hh - mu
        var = jnp.mean(hc * hc, axis=-1, keepdims=True)
        y = hc * lax.rsqrt(var + EPS) * gn_ref[:, cs] + sk_ref[:, cs] * xc[:, cs]
        o_ref[:, cs] = (so_ref[:, cs].astype(F32) * y).astype(BF16)


def _mlstm(p, conv_w, conv_b, bdq, bdk, bdv, wg, bg, gnorm, skip, *, batch, seq, L):
    n = p.shape[0]
    d = D_MODEL
    nt = seq // L
    col = lambda c: pl.BlockSpec((L, d), lambda b, t: (b * nt + t, c))
    full = lambda shp: pl.BlockSpec(shp, lambda b, t: (0,) * len(shp))
    return pl.pallas_call(
        functools.partial(_mlstm_kernel, L=L),
        grid=(batch, nt),
        in_specs=[col(4), col(5), full((MLSTM_CONV, d)), full((1, d)),
                  full(bdq.shape), full(bdk.shape), full(bdv.shape),
                  full(wg.shape), full((1, LANES)), full((1, d)), full((1, d))],
        out_specs=col(0),
        out_shape=jax.ShapeDtypeStruct((n, d), BF16),
        scratch_shapes=[
            pltpu.VMEM((8 + L, d), F32),
            pltpu.VMEM((MLSTM_HEADS, MLSTM_DH, MLSTM_DH), F32),
            pltpu.VMEM((MLSTM_HEADS, 1, MLSTM_DH), F32),
            pltpu.VMEM((MLSTM_HEADS, 1, LANES), F32),
        ],
        compiler_params=_cparams(("parallel", "arbitrary")),
        name="mlstm",
    )(p, p, conv_w, conv_b, bdq, bdk, bdv, wg, bg, gnorm, skip)


def _mix_out_kernel(ya_ref, yb_ref, ga_ref, gb_ref, x_ref, wa_ref, wb_ref, wo_ref, gp_ref, gt_ref,
                    o_ref):
    pa = _dot(ya_ref[...], wa_ref[...])
    pb = _dot(yb_ref[...], wb_ref[...])
    mixed = ga_ref[...].astype(F32) * pa + gb_ref[...].astype(F32) * pb
    y = _dot(mixed.astype(BF16), wo_ref[...])
    ms = jnp.mean(y * y, axis=-1, keepdims=True)
    o_ref[...] = x_ref[...] + gt_ref[0] * (y * lax.rsqrt(ms + EPS) * gp_ref[...])


def _mix_out(ya, yb, p, x2, wa, wb, wo, g_post, mod3, *, seq, tile):
    n, d = x2.shape
    tpr = seq // tile
    row = lambda c: pl.BlockSpec((tile, d), lambda i: (i, c))
    wspec = pl.BlockSpec((d, d), lambda i: (0, 0))
    return pl.pallas_call(
        _mix_out_kernel,
        grid=(n // tile,),
        in_specs=[row(0), row(0), row(6), row(7), row(0), wspec, wspec, wspec,
                  pl.BlockSpec((1, d), lambda i: (0, 0)),
                  pl.BlockSpec((1, 1, d), lambda i: (i // tpr, 0, 2))],
        out_specs=row(0),
        out_shape=jax.ShapeDtypeStruct((n, d), F32),
        compiler_params=_cparams(("parallel",)),
        name="mix_out",
    )(ya, yb, p, p, x2, wa, wb, wo, g_post, mod3)


def _ffn_kernel(x_ref, g_ref, sc_ref, sh_ref, gp_ref, gt_ref, w1_ref, w3_ref, w2_ref, o_ref,
                h_scr, acc_scr):
    j = pl.program_id(1)

    @pl.when(j == 0)
    def _():
        h_scr[...] = _rms_mod(x_ref[...], g_ref[...], sc_ref[0], sh_ref[0]).astype(BF16)
        acc_scr[...] = jnp.zeros_like(acc_scr)

    h = h_scr[...]
    a = _dot(h, w1_ref[...])
    act = (a * _sigmoid(a)) * _dot(h, w3_ref[...])
    acc_scr[...] += _dot(act.astype(BF16), w2_ref[...])

    @pl.when(j == pl.num_programs(1) - 1)
    def _():
        y = acc_scr[...]
        ms = jnp.mean(y * y, axis=-1, keepdims=True)
        o_ref[...] = x_ref[...] + gt_ref[0] * (y * lax.rsqrt(ms + EPS) * gp_ref[...])


def _ffn(x2, g_pre, g_post, mod3, w1, w3, w2, *, seq, tile, tf):
    n, d = x2.shape
    dff = w1.shape[1]
    tpr = seq // tile
    vec = pl.BlockSpec((1, d), lambda i, j: (0, 0))
    modspec = lambda c: pl.BlockSpec((1, 1, d), lambda i, j: (i // tpr, 0, c))
    return pl.pallas_call(
        _ffn_kernel,
        grid=(n // tile, dff // tf),
        in_specs=[pl.BlockSpec((tile, d), lambda i, j: (i, 0)), vec, modspec(4), modspec(3), vec,
                  modspec(5),
                  pl.BlockSpec((d, tf), lambda i, j: (0, j)),
                  pl.BlockSpec((d, tf), lambda i, j: (0, j)),
                  pl.BlockSpec((tf, d), lambda i, j: (j, 0))],
        out_specs=pl.BlockSpec((tile, d), lambda i, j: (i, 0)),
        out_shape=jax.ShapeDtypeStruct((n, d), F32),
        scratch_shapes=[pltpu.VMEM((tile, d), BF16), pltpu.VMEM((tile, d), F32)],
        compiler_params=_cparams(("parallel", "arbitrary")),
        name="ffn",
    )(x2, g_pre, mod3, mod3, g_post, mod3, w1, w3, w2)


def _moe_kernel(x_ref, g_ref, sc_ref, sh_ref, gp_ref, gt_ref, rt_ref, w1_ref, w3_ref, w2_ref,
                o_ref, h_scr, acc_scr, gate_scr):
    e = pl.program_id(1)
    j = pl.program_id(2)
    tile = h_scr.shape[0]
    lane = lax.broadcasted_iota(jnp.int32, (tile, LANES), 1)

    @pl.when((e == 0) & (j == 0))
    def _():
        h = _rms_mod(x_ref[...], g_ref[...], sc_ref[0], sh_ref[0])
        h_scr[...] = h.astype(BF16)
        acc_scr[...] = jnp.zeros_like(acc_scr)
        h_hi = h.astype(BF16)
        h_lo = (h - h_hi.astype(F32)).astype(BF16)
        r = rt_ref[...]
        r_hi = r.astype(BF16)
        r_lo = (r - r_hi.astype(F32)).astype(BF16)
        logits = _dot(h_hi, r_hi) + _dot(h_hi, r_lo) + _dot(h_lo, r_hi)
        logits = jnp.where(lane < N_EXPERTS, logits, NEG)
        lane_f = lane.astype(F32)
        v1 = jnp.max(logits, axis=-1, keepdims=True)
        i1 = jnp.min(jnp.where(logits == v1, lane_f, float(LANES)), axis=-1, keepdims=True)
        m1 = lane_f == i1
        l2 = jnp.where(m1, NEG, logits)
        v2 = jnp.max(l2, axis=-1, keepdims=True)
        i2 = jnp.min(jnp.where(l2 == v2, lane_f, float(LANES)), axis=-1, keepdims=True)
        m2 = lane_f == i2
        e2 = jnp.exp(v2 - v1)
        inv = 1.0 / (1.0 + e2)
        gate_scr[...] = jnp.where(m1, inv, 0.0) + jnp.where(m2, e2 * inv, 0.0)

    h = h_scr[...]
    a = _dot(h, w1_ref[0])
    act = (a * _sigmoid(a)) * _dot(h, w3_ref[0])
    gate_e = jnp.sum(jnp.where(lane == e, gate_scr[...], 0.0), axis=-1, keepdims=True)
    acc_scr[...] += gate_e * _dot(act.astype(BF16), w2_ref[0])

    @pl.when((e == pl.num_programs(1) - 1) & (j == pl.num_programs(2) - 1))
    def _():
        y = acc_scr[...]
        ms = jnp.mean(y * y, axis=-1, keepdims=True)
        o_ref[...] = x_ref[...] + gt_ref[0] * (y * lax.rsqrt(ms + EPS) * gp_ref[...])


def _moe(x2, g_pre, g_post, mod3, router_pad, w1, w3, w2, *, seq, tile, tf):
    n, d = x2.shape
    ne, _, dff = w1.shape
    tpr = seq // tile
    vec = pl.BlockSpec((1, d), lambda i, e, j: (0, 0))
    modspec = lambda c: pl.BlockSpec((1, 1, d), lambda i, e, j: (i // tpr, 0, c))
    return pl.pallas_call(
        _moe_kernel,
        grid=(n // tile, ne, dff // tf),
        in_specs=[pl.BlockSpec((tile, d), lambda i, e, j: (i, 0)), vec, modspec(4), modspec(3), vec,
                  modspec(5),
                  pl.BlockSpec((d, LANES), lambda i, e, j: (0, 0)),
                  pl.BlockSpec((1, d, tf), lambda i, e, j: (e, 0, j)),
                  pl.BlockSpec((1, d, tf), lambda i, e, j: (e, 0, j)),
                  pl.BlockSpec((1, tf, d), lambda i, e, j: (e, j, 0))],
        out_specs=pl.BlockSpec((tile, d), lambda i, e, j: (i, 0)),
        out_shape=jax.ShapeDtypeStruct((n, d), F32),
        scratch_shapes=[pltpu.VMEM((tile, d), BF16), pltpu.VMEM((tile, d), F32),
                        pltpu.VMEM((tile, LANES), F32)],
        compiler_params=_cparams(("parallel", "arbitrary", "arbitrary")),
        name="moe",
    )(x2, g_pre, mod3, mod3, g_post, mod3, router_pad, w1, w3, w2)


def _blockdiag_tiles(w):
    nb = w.shape[0]
    per = LANES // QKV_BLOCK
    wt = w.reshape(nb // per, per, QKV_BLOCK, QKV_BLOCK)
    eye = jnp.eye(per, dtype=w.dtype)
    t = jnp.einsum("gnio,nm->gnimo", wt, eye)
    return t.reshape(nb // per, LANES, LANES).astype(BF16)


def _lower_bounds(lb_raw):
    p = jax.nn.softmax(lb_raw.astype(F32), axis=0)
    return jnp.cumsum(p, axis=0) - p[0:1]


def kernel(x, c, w_ada, b_ada, g_pre_mix, g_post_mix, g_pre_ffn, g_post_ffn, w_in, hgrn_lb, hgrn_gnorm, mlstm_conv_w, mlstm_conv_b, mlstm_wq, mlstm_wk, mlstm_wv, mlstm_w_ig, mlstm_b_ig, mlstm_w_fg, mlstm_b_fg, mlstm_gnorm, mlstm_skip, w_proj_a, w_proj_b, w_out, ffn_w1, ffn_w3, ffn_w2, moe_router, moe_w1, moe_w3, moe_w2):
    batch, seq, d = x.shape
    depth = w_in.shape[0]
    n = batch * seq
    tile = min(512, seq)
    hg_tb = min(256, seq)
    ml_l = min(256, seq)

    bp = -(-batch // 16) * 16
    c_pad = jnp.pad(c, ((0, bp - batch), (0, 0)))
    mod = _ada(c_pad, w_ada, b_ada)
    lower = _lower_bounds(hgrn_lb)

    x2 = x.reshape(n, d)
    row = lambda a: a.reshape(1, -1)
    for l in range(depth):
        mod3 = mod[l].reshape(bp, 1, 6 * d)
        p, lf = _in_proj(x2, row(g_pre_mix[l]), mod3, row(lower[l]), w_in[l].astype(BF16),
                         seq=seq, tile=tile)
        ya = _hgrn(p, lf, row(hgrn_gnorm[l]), batch=batch, seq=seq, tb=hg_tb, chunk=64)
        wg = jnp.concatenate([mlstm_w_ig[l], mlstm_w_fg[l]], axis=-1)
        wg = jnp.pad(wg, ((0, 0), (0, LANES - 2 * MLSTM_HEADS))).astype(BF16)
        bg = jnp.pad(jnp.concatenate([mlstm_b_ig[l], mlstm_b_fg[l]]), (0, LANES - 2 * MLSTM_HEADS))
        yb = _mlstm(p, mlstm_conv_w[l], row(mlstm_conv_b[l]), _blockdiag_tiles(mlstm_wq[l]),
                    _blockdiag_tiles(mlstm_wk[l]), _blockdiag_tiles(mlstm_wv[l]), wg, row(bg),
                    row(mlstm_gnorm[l]), row(mlstm_skip[l]), batch=batch, seq=seq, L=ml_l)
        x2 = _mix_out(ya, yb, p, x2, w_proj_a[l].astype(BF16), w_proj_b[l].astype(BF16),
                      w_out[l].astype(BF16), row(g_post_mix[l]), mod3, seq=seq, tile=tile)
        j = l // 2
        if l % 2 == 0:
            x2 = _ffn(x2, row(g_pre_ffn[l]), row(g_post_ffn[l]), mod3, ffn_w1[j].astype(BF16),
                      ffn_w3[j].astype(BF16), ffn_w2[j].astype(BF16), seq=seq, tile=tile, tf=1408)
        else:
            rpad = jnp.pad(moe_router[j], ((0, 0), (0, LANES - N_EXPERTS)))
            x2 = _moe(x2, row(g_pre_ffn[l]), row(g_post_ffn[l]), mod3, rpad,
                      moe_w1[j].astype(BF16), moe_w3[j].astype(BF16), moe_w2[j].astype(BF16),
                      seq=seq, tile=tile, tf=896)
    return x2.reshape(batch, seq, d)
```

```python
import functools

import jax
import jax.numpy as jnp
from jax import lax
from jax.experimental import pallas as pl
from jax.experimental.pallas import tpu as pltpu

F32 = jnp.float32
BF16 = jnp.bfloat16
EPS = 1e-6
NEG = -1e30
LANES = 128

D_MODEL = 1024
HGRN_HEADS = 8
HGRN_DH = 128
MLSTM_HEADS = 4
MLSTM_DH = 256
MLSTM_CONV = 4
QKV_BLOCK = 4
N_EXPERTS = 8
TOP_K = 2
N_SEG = 8
IN_TILE = 1024
IN_SUB_ROWS = 256

VMEM_LIMIT = 56 * 1024 * 1024


def _cparams(sem):
    return pltpu.CompilerParams(dimension_semantics=sem, vmem_limit_bytes=VMEM_LIMIT)


def _sigmoid(x):
    return 0.5 * jnp.tanh(0.5 * x) + 0.5


def _dot(a, b):
    return jnp.dot(a, b, preferred_element_type=F32)


def _dot_nt(a, b):
    return lax.dot_general(a, b, (((1,), (1,)), ((), ())), preferred_element_type=F32)


def _dot_tn(a, b):
    return lax.dot_general(a, b, (((0,), (0,)), ((), ())), preferred_element_type=F32)


def _split3(x):
    hi = x.astype(BF16)
    r1 = x - hi.astype(F32)
    mid = r1.astype(BF16)
    lo = (r1 - mid.astype(F32)).astype(BF16)
    return hi, mid, lo


def _chunk_tril(n, chunk):
    r = lax.broadcasted_iota(jnp.int32, (n, n), 0)
    c = lax.broadcasted_iota(jnp.int32, (n, n), 1)
    keep = (r >= c) & ((r // chunk) == (c // chunk))
    return jnp.where(keep, 1.0, 0.0).astype(BF16)


def _cumsum_rows(tri, x):
    hi, mid, lo = _split3(x)
    return _dot(tri, hi) + _dot(tri, mid) + _dot(tri, lo)


def _rms_mod(x, g, sc, sh):
    ms = jnp.mean(x * x, axis=-1, keepdims=True)
    return (x * lax.rsqrt(ms + EPS) * g) * (1.0 + sc) + sh


def _ada_kernel(c_ref, w_ref, b_ref, o_ref):
    c = c_ref[...]
    ca = (c * _sigmoid(c)).astype(BF16)
    o_ref[0] = _dot(ca, w_ref[0].astype(BF16)) + b_ref[0]


def _ada(c_pad, w_ada, b_ada):
    depth, d, six_d = w_ada.shape
    bp = c_pad.shape[0]
    tn = 1024
    return pl.pallas_call(
        _ada_kernel,
        grid=(depth, six_d // tn),
        in_specs=[
            pl.BlockSpec((bp, d), lambda l, j: (0, 0)),
            pl.BlockSpec((1, d, tn), lambda l, j: (l, 0, j)),
            pl.BlockSpec((1, 1, tn), lambda l, j: (l, 0, j)),
        ],
        out_specs=pl.BlockSpec((1, bp, tn), lambda l, j: (l, 0, j)),
        out_shape=jax.ShapeDtypeStruct((depth, bp, six_d), F32),
        compiler_params=_cparams(("parallel", "parallel")),
        name="ada_mod",
    )(c_pad, w_ada, b_ada.reshape(depth, 1, six_d))


def _in_kernel(x_ref, g_ref, sc_ref, sh_ref, lb_ref, w_ref, p_ref, lf_ref, h_scr):
    j = pl.program_id(1)

    @pl.when(j == 0)
    def _():
        h_scr[...] = _rms_mod(x_ref[...], g_ref[...], sc_ref[0], sh_ref[0]).astype(BF16)

    def run(epilogue):
        for r0 in range(0, h_scr.shape[0], IN_SUB_ROWS):
            rows = slice(r0, r0 + IN_SUB_ROWS)
            epilogue(rows, _dot(h_scr[rows, :], w_ref[...]))

    def silu(rows, acc):
        p_ref[rows, :] = (acc * _sigmoid(acc)).astype(BF16)

    def forget(rows, acc):
        lb = lb_ref[...]
        sg = _sigmoid(acc)
        lf_ref[rows, :] = jnp.log(lb + (1.0 - lb) * sg)
        p_ref[rows, :] = ((1.0 - lb) * (1.0 - sg)).astype(BF16)

    def ident(rows, acc):
        p_ref[rows, :] = acc.astype(BF16)

    def sigm(rows, acc):
        p_ref[rows, :] = _sigmoid(acc).astype(BF16)

    pl.when((j == 0) | (j == 3))(lambda: run(silu))
    pl.when(j == 1)(lambda: run(forget))
    pl.when((j == 2) | (j == 4))(lambda: run(ident))
    pl.when(j >= 5)(lambda: run(sigm))


def _in_proj(x2, g_pre, mod3, lb, w_in, *, seq, tile):
    n, d = x2.shape
    tiles_per_row = seq // tile
    return pl.pallas_call(
        _in_kernel,
        grid=(n // tile, N_SEG),
        in_specs=[
            pl.BlockSpec((tile, d), lambda i, j: (i, 0)),
            pl.BlockSpec((1, d), lambda i, j: (0, 0)),
            pl.BlockSpec((1, 1, d), lambda i, j: (i // tiles_per_row, 0, 1)),
            pl.BlockSpec((1, 1, d), lambda i, j: (i // tiles_per_row, 0, 0)),
            pl.BlockSpec((1, d), lambda i, j: (0, 0)),
            pl.BlockSpec((d, d), lambda i, j: (0, j)),
        ],
        out_specs=[
            pl.BlockSpec((tile, d), lambda i, j: (i, j)),
            pl.BlockSpec((tile, d), lambda i, j: (i, 0)),
        ],
        out_shape=[
            jax.ShapeDtypeStruct((n, N_SEG * d), BF16),
            jax.ShapeDtypeStruct((n, d), F32),
        ],
        scratch_shapes=[pltpu.VMEM((tile, d), BF16)],
        compiler_params=_cparams(("parallel", "arbitrary")),
        name="in_proj",
    )(x2, g_pre, mod3, mod3, lb, w_in)


def _hgrn_kernel(q_ref, k_ref, v_ref, sg_ref, lf_ref, gn_ref, o_ref, st_scr, *, tb, chunk):
    @pl.when(pl.program_id(1) == 0)
    def _():
        st_scr[...] = jnp.zeros_like(st_scr)

    tri = _chunk_tril(tb, chunk)
    b_all = _cumsum_rows(tri, lf_ref[...])
    r_io = lax.broadcasted_iota(jnp.int32, (chunk, chunk), 0)
    c_io = lax.broadcasted_iota(jnp.int32, (chunk, chunk), 1)
    causal = r_io >= c_io
    mid = chunk // 2 - 1
    for ci in range(tb // chunk):
        r0 = ci * chunk
        rs = slice(r0, r0 + chunk)
        for h in range(HGRN_HEADS):
            cs = slice(h * HGRN_DH, (h + 1) * HGRN_DH)
            b = b_all[rs, cs]
            b_mid = b[mid:mid + 1]
            b_last = b[chunk - 1:chunk]
            v = v_ref[rs, cs]
            qi = q_ref[rs, cs].astype(F32) * jnp.exp(b - b_mid)
            ki = k_ref[rs, cs].astype(F32) * jnp.exp(b_mid - b)
            q_st = (qi * jnp.exp(b_mid)).astype(BF16)
            k_st = (ki * jnp.exp(b_last - b_mid)).astype(BF16)
            sc = jnp.where(causal, _dot_nt(qi.astype(BF16), ki.astype(BF16)), 0.0).astype(BF16)
            st = st_scr[h]
            o = _dot(sc, v) + _dot_nt(q_st, st.astype(BF16))
            st_scr[h] = st * jnp.exp(b_last) + _dot_tn(v, k_st)
            ms = jnp.mean(o * o, axis=-1, keepdims=True)
            y = o * lax.rsqrt(ms + EPS) * gn_ref[:, cs] * sg_ref[rs, cs].astype(F32)
            o_ref[rs, cs] = y.astype(BF16)


def _hgrn(p, lf, gnorm, *, batch, seq, tb, chunk):
    n, d = lf.shape
    nt = seq // tb
    col = lambda c: pl.BlockSpec((tb, d), lambda b, t: (b * nt + t, c))
    return pl.pallas_call(
        functools.partial(_hgrn_kernel, tb=tb, chunk=chunk),
        grid=(batch, nt),
        in_specs=[col(0), col(1), col(2), col(3), col(0),
                  pl.BlockSpec((1, d), lambda b, t: (0, 0))],
        out_specs=col(0),
        out_shape=jax.ShapeDtypeStruct((n, d), BF16),
        scratch_shapes=[pltpu.VMEM((HGRN_HEADS, HGRN_DH, HGRN_DH), F32)],
        compiler_params=_cparams(("parallel", "arbitrary")),
        name="hgrn2",
    )(p, p, p, p, lf, gnorm)


def _mlstm_kernel(xm_ref, so_ref, cw_ref, cb_ref, bdq_ref, bdk_ref, bdv_ref, wg_ref, bg_ref,
                  gn_ref, sk_ref, o_ref, xbuf, c_scr, n_scr, m_scr, *, L):
    d = D_MODEL

    @pl.when(pl.program_id(1) == 0)
    def _():
        xbuf[0:8, :] = jnp.zeros((8, d), F32)
        c_scr[...] = jnp.zeros_like(c_scr)
        n_scr[...] = jnp.zeros_like(n_scr)
        m_scr[...] = jnp.zeros_like(m_scr)

    xmb = xm_ref[...]
    xm = xmb.astype(F32)
    xbuf[8:8 + L, :] = xm
    cw = cw_ref[...]
    conv = (cb_ref[...] + cw[3:4] * xm + cw[2:3] * xbuf[7:7 + L, :]
            + cw[1:2] * xbuf[6:6 + L, :] + cw[0:1] * xbuf[5:5 + L, :])
    xbuf[0:8, :] = xm[L - 8:L]
    xc = conv * _sigmoid(conv)
    xcb = xc.astype(BF16)

    qs, ks, vs = [], [], []
    for g in range(d // LANES):
        gs = slice(g * LANES, (g + 1) * LANES)
        qs.append(_dot(xcb[:, gs], bdq_ref[g]))
        ks.append(_dot(xcb[:, gs], bdk_ref[g]))
        vs.append(_dot(xmb[:, gs], bdv_ref[g]))
    q = jnp.concatenate(qs, axis=-1)
    k = jnp.concatenate(ks, axis=-1)
    v = jnp.concatenate(vs, axis=-1)
    qb, kb, vb = q.astype(BF16), k.astype(BF16), v.astype(BF16)

    gates = (_dot(qb, wg_ref[0:d, :]) + _dot(kb, wg_ref[d:2 * d, :])
             + _dot(vb, wg_ref[2 * d:3 * d, :]) + bg_ref[...])
    lf = jnp.minimum(gates, 0.0) - jnp.log(1.0 + jnp.exp(-jnp.abs(gates)))
    bcs = _cumsum_rows(_chunk_tril(L, L), lf)
    gates_t = gates.T
    bcs_t = bcs.T

    r_io = lax.broadcasted_iota(jnp.int32, (L, L), 0)
    c_io = lax.broadcasted_iota(jnp.int32, (L, L), 1)
    causal = r_io >= c_io
    kscale = MLSTM_DH ** -0.5
    for h in range(MLSTM_HEADS):
        cs = slice(h * MLSTM_DH, (h + 1) * MLSTM_DH)
        a_row = gates_t[h:h + 1, :] - bcs_t[4 + h:5 + h, :]
        b_col = bcs[:, 4 + h:5 + h]
        a_col = gates[:, h:h + 1] - b_col
        m_prev = m_scr[h][:, 0:1]
        am = jnp.where(causal, a_row, NEG)
        m_run = jnp.maximum(jnp.max(am, axis=-1, keepdims=True), m_prev)
        e = jnp.exp(am - m_run)
        qh = q[:, cs]
        khs = k[:, cs] * kscale
        s = _dot_nt(qb[:, cs], khs.astype(BF16)) * e
        w_inter = jnp.exp(m_prev - m_run)
        num = _dot(s.astype(BF16), vb[:, cs]) + w_inter * _dot(qb[:, cs], c_scr[h].astype(BF16))
        den = (jnp.sum(s, axis=-1, keepdims=True)
               + w_inter * jnp.sum(qh * n_scr[h], axis=-1, keepdims=True))
        m_t = b_col + m_run
        hh = num / jnp.maximum(jnp.abs(den), jnp.exp(-m_t))
        m_last = m_run[L - 1:L]
        w_s = jnp.exp(a_col - m_last)
        dec = jnp.exp(m_prev - m_last)
        kw = khs * w_s
        c_scr[h] = dec * c_scr[h] + _dot_tn(kw.astype(BF16), vb[:, cs])
        n_scr[h] = dec * n_scr[h] + jnp.sum(kw, axis=0, keepdims=True)
        m_scr[h] = jnp.broadcast_to(b_col[L - 1:L] + m_last, (1, LANES))
        mu = jnp.mean(hh, axis=-1, keepdims=True)
        hc = hh - mu
        var = jnp.mean(hc * hc, axis=-1, keepdims=True)
        y = hc * lax.rsqrt(var + EPS) * gn_ref[:, cs] + sk_ref[:, cs] * xc[:, cs]
        o_ref[:, cs] = (so_ref[:, cs].astype(F32) * y).astype(BF16)


def _mlstm(p, conv_w, conv_b, bdq, bdk, bdv, wg, bg, gnorm, skip, *, batch, seq, L):
    n = p.shape[0]
    d = D_MODEL
    nt = seq // L
    col = lambda c: pl.BlockSpec((L, d), lambda b, t: (b * nt + t, c))
    full = lambda shp: pl.BlockSpec(shp, lambda b, t: (0,) * len(shp))
    return pl.pallas_call(
        functools.partial(_mlstm_kernel, L=L),
        grid=(batch, nt),
        in_specs=[col(4), col(5), full((MLSTM_CONV, d)), full((1, d)),
                  full(bdq.shape), full(bdk.shape), full(bdv.shape),
                  full(wg.shape), full((1, LANES)), full((1, d)), full((1, d))],
        out_specs=col(0),
        out_shape=jax.ShapeDtypeStruct((n, d), BF16),
        scratch_shapes=[
            pltpu.VMEM((8 + L, d), F32),
            pltpu.VMEM((MLSTM_HEADS, MLSTM_DH, MLSTM_DH), F32),
            pltpu.VMEM((MLSTM_HEADS, 1, MLSTM_DH), F32),
            pltpu.VMEM((MLSTM_HEADS, 1, LANES), F32),
        ],
        compiler_params=_cparams(("parallel", "arbitrary")),
        name="mlstm",
    )(p, p, conv_w, conv_b, bdq, bdk, bdv, wg, bg, gnorm, skip)


def _mix_out_kernel(ya_ref, yb_ref, ga_ref, gb_ref, x_ref, wa_ref, wb_ref, wo_ref, gp_ref, gt_ref,
                    o_ref):
    pa = _dot(ya_ref[...], wa_ref[...])
    pb = _dot(yb_ref[...], wb_ref[...])
    mixed = ga_ref[...].astype(F32) * pa + gb_ref[...].astype(F32) * pb
    y = _dot(mixed.astype(BF16), wo_ref[...])
    ms = jnp.mean(y * y, axis=-1, keepdims=True)
    o_ref[...] = x_ref[...] + gt_ref[0] * (y * lax.rsqrt(ms + EPS) * gp_ref[...])


def _mix_out(ya, yb, p, x2, wa, wb, wo, g_post, mod3, *, seq, tile):
    n, d = x2.shape
    tpr = seq // tile
    row = lambda c: pl.BlockSpec((tile, d), lambda i: (i, c))
    wspec = pl.BlockSpec((d, d), lambda i: (0, 0))
    return pl.pallas_call(
        _mix_out_kernel,
        grid=(n // tile,),
        in_specs=[row(0), row(0), row(6), row(7), row(0), wspec, wspec, wspec,
                  pl.BlockSpec((1, d), lambda i: (0, 0)),
                  pl.BlockSpec((1, 1, d), lambda i: (i // tpr, 0, 2))],
        out_specs=row(0),
        out_shape=jax.ShapeDtypeStruct((n, d), F32),
        compiler_params=_cparams(("parallel",)),
        name="mix_out",
    )(ya, yb, p, p, x2, wa, wb, wo, g_post, mod3)


def _ffn_kernel(x_ref, g_ref, sc_ref, sh_ref, gp_ref, gt_ref, w1_ref, w3_ref, w2_ref, o_ref,
                h_scr, acc_scr):
    j = pl.program_id(1)

    @pl.when(j == 0)
    def _():
        h_scr[...] = _rms_mod(x_ref[...], g_ref[...], sc_ref[0], sh_ref[0]).astype(BF16)
        acc_scr[...] = jnp.zeros_like(acc_scr)

    h = h_scr[...]
    a = _dot(h, w1_ref[...])
    act = (a * _sigmoid(a)) * _dot(h, w3_ref[...])
    acc_scr[...] += _dot(act.astype(BF16), w2_ref[...])

    @pl.when(j == pl.num_programs(1) - 1)
    def _():
        y = acc_scr[...]
        ms = jnp.mean(y * y, axis=-1, keepdims=True)
        o_ref[...] = x_ref[...] + gt_ref[0] * (y * lax.rsqrt(ms + EPS) * gp_ref[...])


def _ffn(x2, g_pre, g_post, mod3, w1, w3, w2, *, seq, tile, tf):
    n, d = x2.shape
    dff = w1.shape[1]
    tpr = seq // tile
    vec = pl.BlockSpec((1, d), lambda i, j: (0, 0))
    modspec = lambda c: pl.BlockSpec((1, 1, d), lambda i, j: (i // tpr, 0, c))
    return pl.pallas_call(
        _ffn_kernel,
        grid=(n // tile, dff // tf),
        in_specs=[pl.BlockSpec((tile, d), lambda i, j: (i, 0)), vec, modspec(4), modspec(3), vec,
                  modspec(5),
                  pl.BlockSpec((d, tf), lambda i, j: (0, j)),
                  pl.BlockSpec((d, tf), lambda i, j: (0, j)),
                  pl.BlockSpec((tf, d), lambda i, j: (j, 0))],
        out_specs=pl.BlockSpec((tile, d), lambda i, j: (i, 0)),
        out_shape=jax.ShapeDtypeStruct((n, d), F32),
        scratch_shapes=[pltpu.VMEM((tile, d), BF16), pltpu.VMEM((tile, d), F32)],
        compiler_params=_cparams(("parallel", "arbitrary")),
        name="ffn",
    )(x2, g_pre, mod3, mod3, g_post, mod3, w1, w3, w2)


SLOT_TILE = 512
DISP_ROWS = 256
COMB_COLS = 256


def _route_kernel(x_ref, g_ref, sc_ref, sh_ref, rt_ref, h_ref, meta_ref, cnt_ref, cnt_scr):
    @pl.when(pl.program_id(0) == 0)
    def _():
        cnt_scr[...] = jnp.zeros_like(cnt_scr)

    tile = x_ref.shape[0]
    lane = lax.broadcasted_iota(jnp.int32, (tile, LANES), 1)
    h = _rms_mod(x_ref[...], g_ref[...], sc_ref[0], sh_ref[0])
    h_hi = h.astype(BF16)
    h_ref[...] = h_hi
    h_lo = (h - h_hi.astype(F32)).astype(BF16)
    r = rt_ref[...]
    r_hi = r.astype(BF16)
    r_lo = (r - r_hi.astype(F32)).astype(BF16)
    logits = _dot(h_hi, r_hi) + _dot(h_hi, r_lo) + _dot(h_lo, r_hi)
    logits = jnp.where(lane < N_EXPERTS, logits, NEG)
    lane_f = lane.astype(F32)
    v1 = jnp.max(logits, axis=-1, keepdims=True)
    i1 = jnp.min(jnp.where(logits == v1, lane_f, float(LANES)), axis=-1, keepdims=True)
    m1 = lane_f == i1
    l2 = jnp.where(m1, NEG, logits)
    v2 = jnp.max(l2, axis=-1, keepdims=True)
    i2 = jnp.min(jnp.where(l2 == v2, lane_f, float(LANES)), axis=-1, keepdims=True)
    m2 = lane_f == i2
    e2 = jnp.exp(v2 - v1)
    inv = 1.0 / (1.0 + e2)
    onehot = jnp.where(m1, 1.0, jnp.where(m2, 1.0, 0.0))
    r_io = lax.broadcasted_iota(jnp.int32, (tile, tile), 0)
    c_io = lax.broadcasted_iota(jnp.int32, (tile, tile), 1)
    before = jnp.where(r_io > c_io, 1.0, 0.0).astype(BF16)
    rank = _dot(before, onehot.astype(BF16)) + cnt_scr[...]
    r1 = jnp.sum(jnp.where(m1, rank, 0.0), axis=-1, keepdims=True)
    r2 = jnp.sum(jnp.where(m2, rank, 0.0), axis=-1, keepdims=True)
    cols = (i1, i2, r1, r2, inv, e2 * inv)
    meta = jnp.zeros((tile, LANES), F32)
    for idx, val in enumerate(cols):
        meta = jnp.where(lane == idx, val, meta)
    meta_ref[...] = meta
    cnt = cnt_scr[...] + jnp.sum(onehot, axis=0, keepdims=True)
    cnt_scr[...] = cnt
    cnt_ref[0] = cnt


def _route(x2, g_pre, mod3, router_pad, *, seq, tile):
    n, d = x2.shape
    tpr = seq // tile
    nt = n // tile
    vec = pl.BlockSpec((1, d), lambda i: (0, 0))
    modspec = lambda c: pl.BlockSpec((1, 1, d), lambda i: (i // tpr, 0, c))
    return pl.pallas_call(
        _route_kernel,
        grid=(nt,),
        in_specs=[pl.BlockSpec((tile, d), lambda i: (i, 0)), vec, modspec(4), modspec(3),
                  pl.BlockSpec((d, LANES), lambda i: (0, 0))],
        out_specs=[pl.BlockSpec((tile, d), lambda i: (i, 0)),
                   pl.BlockSpec((tile, LANES), lambda i: (i, 0)),
                   pl.BlockSpec((1, 1, LANES), lambda i: (i, 0, 0))],
        out_shape=[jax.ShapeDtypeStruct((n, d), BF16),
                   jax.ShapeDtypeStruct((n, LANES), F32),
                   jax.ShapeDtypeStruct((nt, 1, LANES), F32)],
        scratch_shapes=[pltpu.VMEM((1, LANES), F32)],
        compiler_params=_cparams(("arbitrary",)),
        name="moe_route",
    )(x2, g_pre, mod3, mod3, router_pad)


def _dispatch_kernel(m_ref, kb_ref, fl_ref, p1_ref, p2_ref, h_ref, init_ref, o_ref):
    del init_ref
    s = pl.program_id(0)
    flags = fl_ref[s]
    rows = o_ref.shape[0]
    tk = h_ref.shape[0]

    def gathered():
        slot = m_ref[s] * rows + lax.broadcasted_iota(jnp.int32, (rows, tk), 0)
        p1 = p1_ref[0]
        p2 = p2_ref[0]
        sel = jnp.where(slot == p1, 1.0, jnp.where(slot == p2, 1.0, 0.0)).astype(BF16)
        return _dot(sel, h_ref[...]).astype(BF16)

    @pl.when((flags & 3) == 3)
    def _():
        o_ref[...] = gathered()

    @pl.when((flags & 3) == 1)
    def _():
        o_ref[...] += gathered()


def _dispatch(vis_m, vis_kb, vis_fl, pos1_rows, pos2_rows, h, n_slots):
    n, d = h.shape
    tk = pos1_rows.shape[-1]
    hs_init = jnp.zeros((n_slots, d), BF16)
    grid_spec = pltpu.PrefetchScalarGridSpec(
        num_scalar_prefetch=3,
        grid=(vis_m.shape[0],),
        in_specs=[pl.BlockSpec((1, 1, tk), lambda s, m, kb, fl: (kb[s], 0, 0)),
                  pl.BlockSpec((1, 1, tk), lambda s, m, kb, fl: (kb[s], 0, 0)),
                  pl.BlockSpec((tk, d), lambda s, m, kb, fl: (kb[s], 0)),
                  pl.BlockSpec(memory_space=pl.ANY)],
        out_specs=pl.BlockSpec((DISP_ROWS, d), lambda s, m, kb, fl: (m[s], 0)),
    )
    return pl.pallas_call(
        _dispatch_kernel,
        grid_spec=grid_spec,
        out_shape=jax.ShapeDtypeStruct((n_slots, d), BF16),
        input_output_aliases={6: 0},
        compiler_params=_cparams(("arbitrary",)),
        name="moe_dispatch",
    )(vis_m, vis_kb, vis_fl, pos1_rows, pos2_rows, h, hs_init)


def _group_kernel(te_ref, nu_ref, hs_ref, w1_ref, w3_ref, w2_ref, y_ref, acc_scr):
    m = pl.program_id(0)
    j = pl.program_id(1)
    last = pl.num_programs(1) - 1
    used = m < nu_ref[0]

    @pl.when(used)
    def _():
        hs = hs_ref[...]
        a = _dot(hs, w1_ref[0])
        act = (a * _sigmoid(a)) * _dot(hs, w3_ref[0])
        part = _dot(act.astype(BF16), w2_ref[0])

        @pl.when(j == 0)
        def _():
            acc_scr[...] = part

        @pl.when(j > 0)
        def _():
            acc_scr[...] += part

        @pl.when(j == last)
        def _():
            y_ref[...] = acc_scr[...].astype(BF16)

    @pl.when(jnp.logical_not(used) & (j == last))
    def _():
        y_ref[...] = jnp.zeros_like(y_ref)


def _group(tile_expert, n_used, hs, w1, w3, w2, *, tf):
    ns, d = hs.shape
    dff = w1.shape[-1]
    nj = dff // tf

    def wj(m, j, nu):
        return jnp.where(m < nu[0], j, nj - 1)

    grid_spec = pltpu.PrefetchScalarGridSpec(
        num_scalar_prefetch=2,
        grid=(ns // SLOT_TILE, nj),
        in_specs=[pl.BlockSpec((SLOT_TILE, d), lambda m, j, te, nu: (m, 0)),
                  pl.BlockSpec((1, d, tf), lambda m, j, te, nu: (te[m], 0, wj(m, j, nu))),
                  pl.BlockSpec((1, d, tf), lambda m, j, te, nu: (te[m], 0, wj(m, j, nu))),
                  pl.BlockSpec((1, tf, d), lambda m, j, te, nu: (te[m], wj(m, j, nu), 0))],
        out_specs=pl.BlockSpec((SLOT_TILE, d), lambda m, j, te, nu: (m, 0)),
        scratch_shapes=[pltpu.VMEM((SLOT_TILE, d), F32)],
    )
    return pl.pallas_call(
        _group_kernel,
        grid_spec=grid_spec,
        out_shape=jax.ShapeDtypeStruct((ns, d), BF16),
        compiler_params=_cparams(("arbitrary", "arbitrary")),
        name="moe_group",
    )(tile_expert, n_used, hs, w1, w3, w2)


def _combine_kernel(i_ref, sb_ref, fl_ref, meta_ref, off_ref, y_ref, x_ref, gp_ref, gt_ref, o_ref,
                    acc_scr):
    s = pl.program_id(0)
    flags = fl_ref[s]
    tile = meta_ref.shape[0]
    cols = y_ref.shape[0]

    @pl.when((flags & 2) != 0)
    def _():
        acc_scr[...] = jnp.zeros_like(acc_scr)

    @pl.when((flags & 1) != 0)
    def _():
        meta = meta_ref[...]
        lane_f = lax.broadcasted_iota(jnp.int32, (tile, LANES), 1).astype(F32)
        off = off_ref[...]
        pos1 = meta[:, 2:3] + jnp.sum(jnp.where(lane_f == meta[:, 0:1], off, 0.0), axis=-1,
                                      keepdims=True)
        pos2 = meta[:, 3:4] + jnp.sum(jnp.where(lane_f == meta[:, 1:2], off, 0.0), axis=-1,
                                      keepdims=True)
        slot = (sb_ref[s] * cols
                + lax.broadcasted_iota(jnp.int32, (tile, cols), 1)).astype(F32)
        g = jnp.where(slot == pos1, meta[:, 4:5], jnp.where(slot == pos2, meta[:, 5:6], 0.0))
        acc_scr[...] += _dot(g.astype(BF16), y_ref[...])

    @pl.when((flags & 4) != 0)
    def _():
        y = acc_scr[...]
        ms = jnp.mean(y * y, axis=-1, keepdims=True)
        o_ref[...] = x_ref[...] + gt_ref[0] * (y * lax.rsqrt(ms + EPS) * gp_ref[...])


def _combine(vis_i, vis_sb, vis_fl, meta, off_row, y, x2, g_post, mod3, *, seq, tile):
    n, d = x2.shape
    tpr = seq // tile
    grid_spec = pltpu.PrefetchScalarGridSpec(
        num_scalar_prefetch=3,
        grid=(vis_i.shape[0],),
        in_specs=[pl.BlockSpec((tile, LANES), lambda s, i, sb, fl: (i[s], 0)),
                  pl.BlockSpec((1, LANES), lambda s, i, sb, fl: (0, 0)),
                  pl.BlockSpec((COMB_COLS, d), lambda s, i, sb, fl: (sb[s], 0)),
                  pl.BlockSpec((tile, d), lambda s, i, sb, fl: (i[s], 0)),
                  pl.BlockSpec((1, d), lambda s, i, sb, fl: (0, 0)),
                  pl.BlockSpec((1, 1, d), lambda s, i, sb, fl: (i[s] // tpr, 0, 5))],
        out_specs=pl.BlockSpec((tile, d), lambda s, i, sb, fl: (i[s], 0)),
        scratch_shapes=[pltpu.VMEM((tile, d), F32)],
    )
    return pl.pallas_call(
        _combine_kernel,
        grid_spec=grid_spec,
        out_shape=jax.ShapeDtypeStruct((n, d), F32),
        compiler_params=_cparams(("arbitrary",)),
        name="moe_combine",
    )(vis_i, vis_sb, vis_fl, meta, off_row, y, x2, g_post, mod3)


def _visits(lo, cnt, n_steps):
    cum = jnp.cumsum(cnt)
    total = cum[-1]
    step = jnp.arange(n_steps, dtype=jnp.int32)
    sc = jnp.minimum(step, total - 1)
    pair = jnp.searchsorted(cum, sc, side="right").astype(jnp.int32)
    blk = lo[pair] + sc - (cum[pair] - cnt[pair])
    return pair, blk.astype(jnp.int32), step < total, total


def _moe(x2, g_pre, g_post, mod3, router_pad, w1, w3, w2, *, seq, tile, tf):
    n, d = x2.shape
    nt = n // tile
    i32 = jnp.int32
    h, meta, cnt = _route(x2, g_pre, mod3, router_pad, seq=seq, tile=tile)

    cnt_after = cnt[:, 0, :N_EXPERTS].astype(i32)
    cnt_before = jnp.concatenate([jnp.zeros((1, N_EXPERTS), i32), cnt_after[:-1]], axis=0)
    padded = -(-cnt_after[-1] // SLOT_TILE) * SLOT_TILE
    region_end = jnp.cumsum(padded)
    off = region_end - padded
    n_slots = TOP_K * n + N_EXPERTS * SLOT_TILE
    n_slot_tiles = n_slots // SLOT_TILE
    n_used = (region_end[-1] // SLOT_TILE).astype(i32)
    tile_ids = jnp.arange(n_slot_tiles, dtype=i32)
    tile_expert = jnp.searchsorted(region_end, jnp.minimum(tile_ids, n_used - 1) * SLOT_TILE,
                                   side="right").astype(i32)
    start = off[None, :] + cnt_before
    stop = off[None, :] + cnt_after
    nonempty = stop > start

    e1 = meta[:, 0].astype(i32)
    e2 = meta[:, 1].astype(i32)
    pos1 = (off[e1] + meta[:, 2].astype(i32)).reshape(nt, 1, tile)
    pos2 = (off[e2] + meta[:, 3].astype(i32)).reshape(nt, 1, tile)

    lo = (start // DISP_ROWS).T.reshape(-1)
    hi = ((stop - 1) // DISP_ROWS).T.reshape(-1)
    cntv = jnp.where(nonempty.T.reshape(-1), hi - lo + 1, 0)
    n_disp = N_EXPERTS * nt + n_slots // DISP_ROWS
    pair, vis_m, valid, _ = _visits(lo, cntv, n_disp)
    vis_kb = pair % nt
    first = jnp.concatenate([jnp.ones((1,), bool), vis_m[1:] != vis_m[:-1]])
    vis_fl = valid.astype(i32) + 2 * first.astype(i32)
    hs = _dispatch(vis_m, vis_kb, vis_fl, pos1, pos2, h, n_slots)

    y = _group(tile_expert, n_used.reshape(1), hs, w1, w3, w2, tf=tf)

    lo = (start // COMB_COLS).reshape(-1)
    hi = ((stop - 1) // COMB_COLS).reshape(-1)
    cntv = jnp.where(nonempty.reshape(-1), hi - lo + 1, 0)
    n_comb = N_EXPERTS * nt + n_slots // COMB_COLS
    pair, vis_sb, valid, total = _visits(lo, cntv, n_comb)
    vis_i = pair // N_EXPERTS
    step = jnp.arange(n_comb, dtype=i32)
    first = jnp.concatenate([jnp.ones((1,), bool), vis_i[1:] != vis_i[:-1]])
    last = jnp.concatenate([vis_i[1:] != vis_i[:-1], jnp.zeros((1,), bool)]) | (step == total - 1)
    vis_fl = valid.astype(i32) + 2 * (first & valid).astype(i32) + 4 * (last & valid).astype(i32)
    off_row = jnp.pad(off.astype(F32), (0, LANES - N_EXPERTS)).reshape(1, LANES)
    return _combine(vis_i, vis_sb, vis_fl, meta, off_row, y, x2, g_post, mod3, seq=seq, tile=tile)


def _blockdiag_tiles(w):
    nb = w.shape[0]
    per = LANES // QKV_BLOCK
    wt = w.reshape(nb // per, per, QKV_BLOCK, QKV_BLOCK)
    eye = jnp.eye(per, dtype=w.dtype)
    t = jnp.einsum("gnio,nm->gnimo", wt, eye)
    return t.reshape(nb // per, LANES, LANES).astype(BF16)


def _lower_bounds(lb_raw):
    p = jax.nn.softmax(lb_raw.astype(F32), axis=0)
    return jnp.cumsum(p, axis=0) - p[0:1]


def kernel(x, c, w_ada, b_ada, g_pre_mix, g_post_mix, g_pre_ffn, g_post_ffn, w_in, hgrn_lb, hgrn_gnorm, mlstm_conv_w, mlstm_conv_b, mlstm_wq, mlstm_wk, mlstm_wv, mlstm_w_ig, mlstm_b_ig, mlstm_w_fg, mlstm_b_fg, mlstm_gnorm, mlstm_skip, w_proj_a, w_proj_b, w_out, ffn_w1, ffn_w3, ffn_w2, moe_router, moe_w1, moe_w3, moe_w2):
    batch, seq, d = x.shape
    depth = w_in.shape[0]
    n = batch * seq
    tile = min(512, seq)
    hg_tb = min(256, seq)
    ml_l = min(256, seq)

    bp = -(-batch // 16) * 16
    c_pad = jnp.pad(c, ((0, bp - batch), (0, 0)))
    mod = _ada(c_pad, w_ada, b_ada)
    lower = _lower_bounds(hgrn_lb)

    x2 = x.reshape(n, d)
    row = lambda a: a.reshape(1, -1)
    for l in range(depth):
        mod3 = mod[l].reshape(bp, 1, 6 * d)
        p, lf = _in_proj(x2, row(g_pre_mix[l]), mod3, row(lower[l]), w_in[l].astype(BF16),
                         seq=seq, tile=min(IN_TILE, seq))
        ya = _hgrn(p, lf, row(hgrn_gnorm[l]), batch=batch, seq=seq, tb=hg_tb, chunk=64)
        wg = jnp.concatenate([mlstm_w_ig[l], mlstm_w_fg[l]], axis=-1)
        wg = jnp.pad(wg, ((0, 0), (0, LANES - 2 * MLSTM_HEADS))).astype(BF16)
        bg = jnp.pad(jnp.concatenate([mlstm_b_ig[l], mlstm_b_fg[l]]), (0, LANES - 2 * MLSTM_HEADS))
        yb = _mlstm(p, mlstm_conv_w[l], row(mlstm_conv_b[l]), _blockdiag_tiles(mlstm_wq[l]),
                    _blockdiag_tiles(mlstm_wk[l]), _blockdiag_tiles(mlstm_wv[l]), wg, row(bg),
                    row(mlstm_gnorm[l]), row(mlstm_skip[l]), batch=batch, seq=seq, L=ml_l)
        x2 = _mix_out(ya, yb, p, x2, w_proj_a[l].astype(BF16), w_proj_b[l].astype(BF16),
                      w_out[l].astype(BF16), row(g_post_mix[l]), mod3, seq=seq, tile=tile)
        j = l // 2
        if l % 2 == 0:
            x2 = _ffn(x2, row(g_pre_ffn[l]), row(g_post_ffn[l]), mod3, ffn_w1[j].astype(BF16),
                      ffn_w3[j].astype(BF16), ffn_w2[j].astype(BF16), seq=seq, tile=tile, tf=1408)
        else:
            rpad = jnp.pad(moe_router[j], ((0, 0), (0, LANES - N_EXPERTS)))
            x2 = _moe(x2, row(g_pre_ffn[l]), row(g_post_ffn[l]), mod3, rpad,
                      moe_w1[j].astype(BF16), moe_w3[j].astype(BF16), moe_w2[j].astype(BF16),
                      seq=seq, tile=tile, tf=1792)
    return x2.reshape(batch, seq, d)
```

```python
import functools

import jax
import jax.numpy as jnp
from jax import lax
from jax.experimental import pallas as pl
from jax.experimental.pallas import tpu as pltpu

F32 = jnp.float32
BF16 = jnp.bfloat16
EPS = 1e-6
NEG = -1e30
LANES = 128

D_MODEL = 1024
HGRN_HEADS = 8
HGRN_DH = 128
MLSTM_HEADS = 4
MLSTM_DH = 256
MLSTM_CONV = 4
QKV_BLOCK = 4
N_EXPERTS = 8
TOP_K = 2
N_SEG = 8
IN_TILE = 1024
IN_SUB_ROWS = 256

VMEM_LIMIT = 56 * 1024 * 1024


def _cparams(sem):
    return pltpu.CompilerParams(dimension_semantics=sem, vmem_limit_bytes=VMEM_LIMIT)


def _sigmoid(x):
    return 0.5 * jnp.tanh(0.5 * x) + 0.5


def _dot(a, b):
    return jnp.dot(a, b, preferred_element_type=F32)


def _dot_nt(a, b):
    return lax.dot_general(a, b, (((1,), (1,)), ((), ())), preferred_element_type=F32)


def _dot_tn(a, b):
    return lax.dot_general(a, b, (((0,), (0,)), ((), ())), preferred_element_type=F32)


def _split3(x):
    hi = x.astype(BF16)
    r1 = x - hi.astype(F32)
    mid = r1.astype(BF16)
    lo = (r1 - mid.astype(F32)).astype(BF16)
    return hi, mid, lo


def _chunk_tril(n, chunk):
    r = lax.broadcasted_iota(jnp.int32, (n, n), 0)
    c = lax.broadcasted_iota(jnp.int32, (n, n), 1)
    keep = (r >= c) & ((r // chunk) == (c // chunk))
    return jnp.where(keep, 1.0, 0.0).astype(BF16)


def _cumsum_rows(tri, x):
    hi, mid, lo = _split3(x)
    return _dot(tri, hi) + _dot(tri, mid) + _dot(tri, lo)


def _rms_mod(x, g, sc, sh):
    ms = jnp.mean(x * x, axis=-1, keepdims=True)
    return (x * lax.rsqrt(ms + EPS) * g) * (1.0 + sc) + sh


def _ada_kernel(c_ref, w_ref, b_ref, o_ref):
    c = c_ref[...]
    ca = (c * _sigmoid(c)).astype(BF16)
    o_ref[0] = _dot(ca, w_ref[0].astype(BF16)) + b_ref[0]


def _ada(c_pad, w_ada, b_ada):
    depth, d, six_d = w_ada.shape
    bp = c_pad.shape[0]
    tn = 1024
    return pl.pallas_call(
        _ada_kernel,
        grid=(depth, six_d // tn),
        in_specs=[
            pl.BlockSpec((bp, d), lambda l, j: (0, 0)),
            pl.BlockSpec((1, d, tn), lambda l, j: (l, 0, j)),
            pl.BlockSpec((1, 1, tn), lambda l, j: (l, 0, j)),
        ],
        out_specs=pl.BlockSpec((1, bp, tn), lambda l, j: (l, 0, j)),
        out_shape=jax.ShapeDtypeStruct((depth, bp, six_d), F32),
        compiler_params=_cparams(("parallel", "parallel")),
        name="ada_mod",
    )(c_pad, w_ada, b_ada.reshape(depth, 1, six_d))


def _in_kernel(x_ref, g_ref, sc_ref, sh_ref, lb_ref, w_ref, p_ref, lf_ref, h_scr):
    j = pl.program_id(1)

    @pl.when(j == 0)
    def _():
        h_scr[...] = _rms_mod(x_ref[...], g_ref[...], sc_ref[0], sh_ref[0]).astype(BF16)

    def run(epilogue):
        for r0 in range(0, h_scr.shape[0], IN_SUB_ROWS):
            rows = slice(r0, r0 + IN_SUB_ROWS)
            epilogue(rows, _dot(h_scr[rows, :], w_ref[...]))

    def silu(rows, acc):
        p_ref[rows, :] = (acc * _sigmoid(acc)).astype(BF16)

    def forget(rows, acc):
        lb = lb_ref[...]
        sg = _sigmoid(acc)
        lf_ref[rows, :] = jnp.log(lb + (1.0 - lb) * sg)
        p_ref[rows, :] = ((1.0 - lb) * (1.0 - sg)).astype(BF16)

    def ident(rows, acc):
        p_ref[rows, :] = acc.astype(BF16)

    def sigm(rows, acc):
        p_ref[rows, :] = _sigmoid(acc).astype(BF16)

    pl.when((j == 0) | (j == 3))(lambda: run(silu))
    pl.when(j == 1)(lambda: run(forget))
    pl.when((j == 2) | (j == 4))(lambda: run(ident))
    pl.when(j >= 5)(lambda: run(sigm))


def _in_proj(x2, g_pre, mod3, lb, w_in, layer, *, seq, tile):
    n, d = x2.shape
    tiles_per_row = seq // tile
    return pl.pallas_call(
        _in_kernel,
        grid=(n // tile, N_SEG),
        in_specs=[
            pl.BlockSpec((tile, d), lambda i, j: (i, 0)),
            pl.BlockSpec((1, d), lambda i, j: (0, 0)),
            pl.BlockSpec((1, 1, d), lambda i, j: (i // tiles_per_row, 0, 1)),
            pl.BlockSpec((1, 1, d), lambda i, j: (i // tiles_per_row, 0, 0)),
            pl.BlockSpec((1, d), lambda i, j: (0, 0)),
            pl.BlockSpec((None, d, d), lambda i, j: (layer, 0, j)),
        ],
        out_specs=[
            pl.BlockSpec((tile, d), lambda i, j: (i, j)),
            pl.BlockSpec((tile, d), lambda i, j: (i, 0)),
        ],
        out_shape=[
            jax.ShapeDtypeStruct((n, N_SEG * d), BF16),
            jax.ShapeDtypeStruct((n, d), F32),
        ],
        scratch_shapes=[pltpu.VMEM((tile, d), BF16)],
        compiler_params=_cparams(("parallel", "arbitrary")),
        name="in_proj",
    )(x2, g_pre, mod3, mod3, lb, w_in)


def _hgrn_kernel(q_ref, k_ref, v_ref, sg_ref, lf_ref, gn_ref, o_ref, st_scr, *, tb, chunk):
    @pl.when(pl.program_id(1) == 0)
    def _():
        st_scr[...] = jnp.zeros_like(st_scr)

    tri = _chunk_tril(tb, chunk)
    b_all = _cumsum_rows(tri, lf_ref[...])
    r_io = lax.broadcasted_iota(jnp.int32, (chunk, chunk), 0)
    c_io = lax.broadcasted_iota(jnp.int32, (chunk, chunk), 1)
    causal = r_io >= c_io
    mid = chunk // 2 - 1
    for ci in range(tb // chunk):
        r0 = ci * chunk
        rs = slice(r0, r0 + chunk)
        for h in range(HGRN_HEADS):
            cs = slice(h * HGRN_DH, (h + 1) * HGRN_DH)
            b = b_all[rs, cs]
            b_mid = b[mid:mid + 1]
            b_last = b[chunk - 1:chunk]
            v = v_ref[rs, cs]
            qi = q_ref[rs, cs].astype(F32) * jnp.exp(b - b_mid)
            ki = k_ref[rs, cs].astype(F32) * jnp.exp(b_mid - b)
            q_st = (qi * jnp.exp(b_mid)).astype(BF16)
            k_st = (ki * jnp.exp(b_last - b_mid)).astype(BF16)
            sc = jnp.where(causal, _dot_nt(qi.astype(BF16), ki.astype(BF16)), 0.0).astype(BF16)
            st = st_scr[h]
            o = _dot(sc, v) + _dot_nt(q_st, st.astype(BF16))
            st_scr[h] = st * jnp.exp(b_last) + _dot_tn(v, k_st)
            ms = jnp.mean(o * o, axis=-1, keepdims=True)
            y = o * lax.rsqrt(ms + EPS) * gn_ref[:, cs] * sg_ref[rs, cs].astype(F32)
            o_ref[rs, cs] = y.astype(BF16)


def _hgrn(p, lf, gnorm, *, batch, seq, tb, chunk):
    n, d = lf.shape
    nt = seq // tb
    col = lambda c: pl.BlockSpec((tb, d), lambda b, t: (b * nt + t, c))
    return pl.pallas_call(
        functools.partial(_hgrn_kernel, tb=tb, chunk=chunk),
        grid=(batch, nt),
        in_specs=[col(0), col(1), col(2), col(3), col(0),
                  pl.BlockSpec((1, d), lambda b, t: (0, 0))],
        out_specs=col(0),
        out_shape=jax.ShapeDtypeStruct((n, d), BF16),
        scratch_shapes=[pltpu.VMEM((HGRN_HEADS, HGRN_DH, HGRN_DH), F32)],
        compiler_params=_cparams(("parallel", "arbitrary")),
        name="hgrn2",
    )(p, p, p, p, lf, gnorm)


def _mlstm_kernel(xm_ref, so_ref, cw_ref, cb_ref, bdq_ref, bdk_ref, bdv_ref, wg_ref, bg_ref,
                  gn_ref, sk_ref, o_ref, xbuf, c_scr, n_scr, m_scr, *, L):
    d = D_MODEL

    @pl.when(pl.program_id(1) == 0)
    def _():
        xbuf[0:8, :] = jnp.zeros((8, d), F32)
        c_scr[...] = jnp.zeros_like(c_scr)
        n_scr[...] = jnp.zeros_like(n_scr)
        m_scr[...] = jnp.zeros_like(m_scr)

    xmb = xm_ref[...]
    xm = xmb.astype(F32)
    xbuf[8:8 + L, :] = xm
    cw = cw_ref[...]
    conv = (cb_ref[...] + cw[3:4] * xm + cw[2:3] * xbuf[7:7 + L, :]
            + cw[1:2] * xbuf[6:6 + L, :] + cw[0:1] * xbuf[5:5 + L, :])
    xbuf[0:8, :] = xm[L - 8:L]
    xc = conv * _sigmoid(conv)
    xcb = xc.astype(BF16)

    qs, ks, vs = [], [], []
    for g in range(d // LANES):
        gs = slice(g * LANES, (g + 1) * LANES)
        qs.append(_dot(xcb[:, gs], bdq_ref[g]))
        ks.append(_dot(xcb[:, gs], bdk_ref[g]))
        vs.append(_dot(xmb[:, gs], bdv_ref[g]))
    q = jnp.concatenate(qs, axis=-1)
    k = jnp.concatenate(ks, axis=-1)
    v = jnp.concatenate(vs, axis=-1)
    qb, kb, vb = q.astype(BF16), k.astype(BF16), v.astype(BF16)

    gates = (_dot(qb, wg_ref[0:d, :]) + _dot(kb, wg_ref[d:2 * d, :])
             + _dot(vb, wg_ref[2 * d:3 * d, :]) + bg_ref[...])
    lf = jnp.minimum(gates, 0.0) - jnp.log(1.0 + jnp.exp(-jnp.abs(gates)))
    bcs = _cumsum_rows(_chunk_tril(L, L), lf)
    gates_t = gates.T
    bcs_t = bcs.T

    r_io = lax.broadcasted_iota(jnp.int32, (L, L), 0)
    c_io = lax.broadcasted_iota(jnp.int32, (L, L), 1)
    causal = r_io >= c_io
    kscale = MLSTM_DH ** -0.5
    for h in range(MLSTM_HEADS):
        cs = slice(h * MLSTM_DH, (h + 1) * MLSTM_DH)
        a_row = gates_t[h:h + 1, :] - bcs_t[4 + h:5 + h, :]
        b_col = bcs[:, 4 + h:5 + h]
        a_col = gates[:, h:h + 1] - b_col
        m_prev = m_scr[h][:, 0:1]
        am = jnp.where(causal, a_row, NEG)
        m_run = jnp.maximum(jnp.max(am, axis=-1, keepdims=True), m_prev)
        e = jnp.exp(am - m_run)
        qh = q[:, cs]
        khs = k[:, cs] * kscale
        s = _dot_nt(qb[:, cs], khs.astype(BF16)) * e
        w_inter = jnp.exp(m_prev - m_run)
        num = _dot(s.astype(BF16), vb[:, cs]) + w_inter * _dot(qb[:, cs], c_scr[h].astype(BF16))
        den = (jnp.sum(s, axis=-1, keepdims=True)
               + w_inter * jnp.sum(qh * n_scr[h], axis=-1, keepdims=True))
        m_t = b_col + m_run
        hh = num / jnp.maximum(jnp.abs(den), jnp.exp(-m_t))
        m_last = m_run[L - 1:L]
        w_s = jnp.exp(a_col - m_last)
        dec = jnp.exp(m_prev - m_last)
        kw = khs * w_s
        c_scr[h] = dec * c_scr[h] + _dot_tn(kw.astype(BF16), vb[:, cs])
        n_scr[h] = dec * n_scr[h] + jnp.sum(kw, axis=0, keepdims=True)
        m_scr[h] = jnp.broadcast_to(b_col[L - 1:L] + m_last, (1, LANES))
        mu = jnp.mean(hh, axis=-1, keepdims=True)
        hc = hh - mu
        var = jnp.mean(hc * hc, axis=-1, keepdims=True)
        y = hc * lax.rsqrt(var + EPS) * gn_ref[:, cs] + sk_ref[:, cs] * xc[:, cs]
        o_ref[:, cs] = (so_ref[:, cs].astype(F32) * y).astype(BF16)


def _mlstm(p, conv_w, conv_b, bdq, bdk, bdv, wg, bg, gnorm, skip, *, batch, seq, L):
    n = p.shape[0]
    d = D_MODEL
    nt = seq // L
    col = lambda c: pl.BlockSpec((L, d), lambda b, t: (b * nt + t, c))
    full = lambda shp: pl.BlockSpec(shp, lambda b, t: (0,) * len(shp))
    return pl.pallas_call(
        functools.partial(_mlstm_kernel, L=L),
        grid=(batch, nt),
        in_specs=[col(4), col(5), full((MLSTM_CONV, d)), full((1, d)),
                  full(bdq.shape), full(bdk.shape), full(bdv.shape),
                  full(wg.shape), full((1, LANES)), full((1, d)), full((1, d))],
        out_specs=col(0),
        out_shape=jax.ShapeDtypeStruct((n, d), BF16),
        scratch_shapes=[
            pltpu.VMEM((8 + L, d), F32),
            pltpu.VMEM((MLSTM_HEADS, MLSTM_DH, MLSTM_DH), F32),
            pltpu.VMEM((MLSTM_HEADS, 1, MLSTM_DH), F32),
            pltpu.VMEM((MLSTM_HEADS, 1, LANES), F32),
        ],
        compiler_params=_cparams(("parallel", "arbitrary")),
        name="mlstm",
    )(p, p, conv_w, conv_b, bdq, bdk, bdv, wg, bg, gnorm, skip)


def _mix_out_kernel(ya_ref, yb_ref, ga_ref, gb_ref, x_ref, wa_ref, wb_ref, wo_ref, gp_ref, gt_ref,
                    o_ref):
    pa = _dot(ya_ref[...], wa_ref[...])
    pb = _dot(yb_ref[...], wb_ref[...])
    mixed = ga_ref[...].astype(F32) * pa + gb_ref[...].astype(F32) * pb
    y = _dot(mixed.astype(BF16), wo_ref[...])
    ms = jnp.mean(y * y, axis=-1, keepdims=True)
    o_ref[...] = x_ref[...] + gt_ref[0] * (y * lax.rsqrt(ms + EPS) * gp_ref[...])


def _mix_out(ya, yb, p, x2, wa, wb, wo, g_post, mod3, layer, *, seq, tile):
    n, d = x2.shape
    tpr = seq // tile
    row = lambda c: pl.BlockSpec((tile, d), lambda i: (i, c))
    wspec = pl.BlockSpec((None, d, d), lambda i: (layer, 0, 0))
    return pl.pallas_call(
        _mix_out_kernel,
        grid=(n // tile,),
        in_specs=[row(0), row(0), row(6), row(7), row(0), wspec, wspec, wspec,
                  pl.BlockSpec((1, d), lambda i: (0, 0)),
                  pl.BlockSpec((1, 1, d), lambda i: (i // tpr, 0, 2))],
        out_specs=row(0),
        out_shape=jax.ShapeDtypeStruct((n, d), F32),
        compiler_params=_cparams(("parallel",)),
        name="mix_out",
    )(ya, yb, p, p, x2, wa, wb, wo, g_post, mod3)


def _ffn_kernel(x_ref, g_ref, sc_ref, sh_ref, gp_ref, gt_ref, w1_ref, w3_ref, w2_ref, o_ref,
                h_scr, acc_scr):
    j = pl.program_id(1)

    @pl.when(j == 0)
    def _():
        h_scr[...] = _rms_mod(x_ref[...], g_ref[...], sc_ref[0], sh_ref[0]).astype(BF16)
        acc_scr[...] = jnp.zeros_like(acc_scr)

    h = h_scr[...]
    a = _dot(h, w1_ref[...])
    act = (a * _sigmoid(a)) * _dot(h, w3_ref[...])
    acc_scr[...] += _dot(act.astype(BF16), w2_ref[...])

    @pl.when(j == pl.num_programs(1) - 1)
    def _():
        y = acc_scr[...]
        ms = jnp.mean(y * y, axis=-1, keepdims=True)
        o_ref[...] = x_ref[...] + gt_ref[0] * (y * lax.rsqrt(ms + EPS) * gp_ref[...])


def _ffn(x2, g_pre, g_post, mod3, w1, w3, w2, layer, *, seq, tile, tf):
    n, d = x2.shape
    dff = w1.shape[-1]
    tpr = seq // tile
    vec = pl.BlockSpec((1, d), lambda i, j: (0, 0))
    modspec = lambda c: pl.BlockSpec((1, 1, d), lambda i, j: (i // tpr, 0, c))
    return pl.pallas_call(
        _ffn_kernel,
        grid=(n // tile, dff // tf),
        in_specs=[pl.BlockSpec((tile, d), lambda i, j: (i, 0)), vec, modspec(4), modspec(3), vec,
                  modspec(5),
                  pl.BlockSpec((None, d, tf), lambda i, j: (layer, 0, j)),
                  pl.BlockSpec((None, d, tf), lambda i, j: (layer, 0, j)),
                  pl.BlockSpec((None, tf, d), lambda i, j: (layer, j, 0))],
        out_specs=pl.BlockSpec((tile, d), lambda i, j: (i, 0)),
        out_shape=jax.ShapeDtypeStruct((n, d), F32),
        scratch_shapes=[pltpu.VMEM((tile, d), BF16), pltpu.VMEM((tile, d), F32)],
        compiler_params=_cparams(("parallel", "arbitrary")),
        name="ffn",
    )(x2, g_pre, mod3, mod3, g_post, mod3, w1, w3, w2)


SLOT_TILE = 512
SEG_ALIGN = 16


def _route_kernel(x_ref, g_ref, sc_ref, sh_ref, rt_ref, h_ref, meta_ref, cnt_ref):
    tile = x_ref.shape[0]
    lane = lax.broadcasted_iota(jnp.int32, (tile, LANES), 1)
    h = _rms_mod(x_ref[...], g_ref[...], sc_ref[0], sh_ref[0])
    h_hi = h.astype(BF16)
    h_ref[...] = h_hi
    h_lo = (h - h_hi.astype(F32)).astype(BF16)
    r = rt_ref[...]
    r_hi = r.astype(BF16)
    r_lo = (r - r_hi.astype(F32)).astype(BF16)
    logits = _dot(h_hi, r_hi) + _dot(h_hi, r_lo) + _dot(h_lo, r_hi)
    logits = jnp.where(lane < N_EXPERTS, logits, NEG)
    lane_f = lane.astype(F32)
    v1 = jnp.max(logits, axis=-1, keepdims=True)
    i1 = jnp.min(jnp.where(logits == v1, lane_f, float(LANES)), axis=-1, keepdims=True)
    m1 = lane_f == i1
    l2 = jnp.where(m1, NEG, logits)
    v2 = jnp.max(l2, axis=-1, keepdims=True)
    i2 = jnp.min(jnp.where(l2 == v2, lane_f, float(LANES)), axis=-1, keepdims=True)
    m2 = lane_f == i2
    e2 = jnp.exp(v2 - v1)
    inv = 1.0 / (1.0 + e2)
    onehot = jnp.where(m1, 1.0, jnp.where(m2, 1.0, 0.0))
    r_io = lax.broadcasted_iota(jnp.int32, (tile, tile), 0)
    c_io = lax.broadcasted_iota(jnp.int32, (tile, tile), 1)
    before = jnp.where(r_io > c_io, 1.0, 0.0).astype(BF16)
    rank = _dot(before, onehot.astype(BF16))
    r1 = jnp.sum(jnp.where(m1, rank, 0.0), axis=-1, keepdims=True)
    r2 = jnp.sum(jnp.where(m2, rank, 0.0), axis=-1, keepdims=True)
    cols = (i1, i2, r1, r2, inv, e2 * inv)
    meta = jnp.zeros((tile, LANES), F32)
    for idx, val in enumerate(cols):
        meta = jnp.where(lane == idx, val, meta)
    meta_ref[...] = meta
    cnt_ref[0] = jnp.sum(onehot, axis=0, keepdims=True)


def _route(x2, g_pre, mod3, router_pad, *, seq, tile):
    n, d = x2.shape
    tpr = seq // tile
    nt = n // tile
    vec = pl.BlockSpec((1, d), lambda i: (0, 0))
    modspec = lambda c: pl.BlockSpec((1, 1, d), lambda i: (i // tpr, 0, c))
    return pl.pallas_call(
        _route_kernel,
        grid=(nt,),
        in_specs=[pl.BlockSpec((tile, d), lambda i: (i, 0)), vec, modspec(4), modspec(3),
                  pl.BlockSpec((d, LANES), lambda i: (0, 0))],
        out_specs=[pl.BlockSpec((tile, d), lambda i: (i, 0)),
                   pl.BlockSpec((tile, LANES), lambda i: (i, 0)),
                   pl.BlockSpec((1, 1, LANES), lambda i: (i, 0, 0))],
        out_shape=[jax.ShapeDtypeStruct((n, d), BF16),
                   jax.ShapeDtypeStruct((n, LANES), F32),
                   jax.ShapeDtypeStruct((nt, 1, LANES), F32)],
        compiler_params=_cparams(("parallel",)),
        name="moe_route",
    )(x2, g_pre, mod3, mod3, router_pad)


def _segment_dma(start_ref, len_ref, i, hbm_ref, buf_ref, sem, *, to_hbm, wait):
    loc = jnp.int32(0)
    for e in range(N_EXPERTS):
        base = start_ref[i * N_EXPERTS + e]
        length = len_ref[i * N_EXPERTS + e]

        def chunk(c, carry, base=base, loc=loc):
            hb = hbm_ref.at[pl.ds(pl.multiple_of(base + c * SEG_ALIGN, SEG_ALIGN), SEG_ALIGN)]
            vb = buf_ref.at[pl.ds(pl.multiple_of(loc + c * SEG_ALIGN, SEG_ALIGN), SEG_ALIGN)]
            cp = pltpu.make_async_copy(vb, hb, sem) if to_hbm else pltpu.make_async_copy(hb, vb, sem)
            if wait:
                cp.wait()
            else:
                cp.start()
            return carry

        lax.fori_loop(0, length // SEG_ALIGN, chunk, 0)
        loc = loc + length


def _dispatch_kernel(start_ref, len_ref, p1_ref, p2_ref, h_ref, init_ref, hs_ref, buf, sem):
    del init_ref
    i = pl.program_id(0)
    rows = buf.shape[0]
    tk = h_ref.shape[0]
    slot = lax.broadcasted_iota(jnp.int32, (rows, tk), 0)
    sel = jnp.where(slot == p1_ref[0], 1.0, jnp.where(slot == p2_ref[0], 1.0, 0.0)).astype(BF16)
    buf[...] = _dot(sel, h_ref[...]).astype(BF16)
    _segment_dma(start_ref, len_ref, i, hs_ref, buf, sem, to_hbm=True, wait=False)
    _segment_dma(start_ref, len_ref, i, hs_ref, buf, sem, to_hbm=True, wait=True)


def _dispatch(seg_start, seg_len, lpos1_rows, lpos2_rows, h, n_slots, local_rows):
    n, d = h.shape
    tk = lpos1_rows.shape[-1]
    hs_init = jnp.zeros((n_slots, d), BF16)
    grid_spec = pltpu.PrefetchScalarGridSpec(
        num_scalar_prefetch=2,
        grid=(n // tk,),
        in_specs=[pl.BlockSpec((1, 1, tk), lambda i, st, ln: (i, 0, 0)),
                  pl.BlockSpec((1, 1, tk), lambda i, st, ln: (i, 0, 0)),
                  pl.BlockSpec((tk, d), lambda i, st, ln: (i, 0)),
                  pl.BlockSpec(memory_space=pl.ANY)],
        out_specs=pl.BlockSpec(memory_space=pl.ANY),
        scratch_shapes=[pltpu.VMEM((local_rows, d), BF16), pltpu.SemaphoreType.DMA(())],
    )
    return pl.pallas_call(
        _dispatch_kernel,
        grid_spec=grid_spec,
        out_shape=jax.ShapeDtypeStruct((n_slots, d), BF16),
        input_output_aliases={5: 0},
        compiler_params=_cparams(("arbitrary",)),
        name="moe_dispatch",
    )(seg_start, seg_len, lpos1_rows, lpos2_rows, h, hs_init)


def _group_kernel(te_ref, nu_ref, hs_ref, w1_ref, w3_ref, w2_ref, y_ref, acc_scr):
    m = pl.program_id(0)
    j = pl.program_id(1)
    last = pl.num_programs(1) - 1
    used = m < nu_ref[0]

    @pl.when(used)
    def _():
        hs = hs_ref[...]
        a = _dot(hs, w1_ref[0])
        act = (a * _sigmoid(a)) * _dot(hs, w3_ref[0])
        part = _dot(act.astype(BF16), w2_ref[0])

        @pl.when(j == 0)
        def _():
            acc_scr[...] = part

        @pl.when(j > 0)
        def _():
            acc_scr[...] += part

        @pl.when(j == last)
        def _():
            y_ref[...] = acc_scr[...].astype(BF16)

    @pl.when(jnp.logical_not(used) & (j == last))
    def _():
        y_ref[...] = jnp.zeros_like(y_ref)


def _group(tile_expert, n_used, hs, w1, w3, w2, layer, *, tf):
    ns, d = hs.shape
    dff = w1.shape[-1]
    nj = dff // tf

    def wj(m, j, nu):
        return jnp.where(m < nu[0], j, nj - 1)

    grid_spec = pltpu.PrefetchScalarGridSpec(
        num_scalar_prefetch=2,
        grid=(ns // SLOT_TILE, nj),
        in_specs=[pl.BlockSpec((SLOT_TILE, d), lambda m, j, te, nu: (m, 0)),
                  pl.BlockSpec((None, 1, d, tf),
                               lambda m, j, te, nu: (layer, te[m], 0, wj(m, j, nu))),
                  pl.BlockSpec((None, 1, d, tf),
                               lambda m, j, te, nu: (layer, te[m], 0, wj(m, j, nu))),
                  pl.BlockSpec((None, 1, tf, d),
                               lambda m, j, te, nu: (layer, te[m], wj(m, j, nu), 0))],
        out_specs=pl.BlockSpec((SLOT_TILE, d), lambda m, j, te, nu: (m, 0)),
        scratch_shapes=[pltpu.VMEM((SLOT_TILE, d), F32)],
    )
    return pl.pallas_call(
        _group_kernel,
        grid_spec=grid_spec,
        out_shape=jax.ShapeDtypeStruct((ns, d), BF16),
        compiler_params=_cparams(("arbitrary", "arbitrary")),
        name="moe_group",
    )(tile_expert, n_used, hs, w1, w3, w2)


def _combine_kernel(start_ref, len_ref, meta_ref, y_ref, x_ref, gp_ref, gt_ref, o_ref, ybuf, sem):
    i = pl.program_id(0)
    tile = meta_ref.shape[0]
    rows = ybuf.shape[0]

    @pl.when(i == 0)
    def _():
        ybuf[...] = jnp.zeros_like(ybuf)

    _segment_dma(start_ref, len_ref, i, y_ref, ybuf, sem, to_hbm=False, wait=False)

    lane1 = lax.broadcasted_iota(jnp.int32, (1, LANES), 1)
    loc_row = jnp.zeros((1, LANES), F32)
    loc = jnp.int32(0)
    for e in range(N_EXPERTS):
        loc_row = jnp.where(lane1 == e, loc.astype(F32), loc_row)
        loc = loc + len_ref[i * N_EXPERTS + e]
    meta = meta_ref[...]
    lane_f = lax.broadcasted_iota(jnp.int32, (tile, LANES), 1).astype(F32)
    pos1 = meta[:, 2:3] + jnp.sum(jnp.where(lane_f == meta[:, 0:1], loc_row, 0.0), axis=-1,
                                  keepdims=True)
    pos2 = meta[:, 3:4] + jnp.sum(jnp.where(lane_f == meta[:, 1:2], loc_row, 0.0), axis=-1,
                                  keepdims=True)
    slot = lax.broadcasted_iota(jnp.int32, (tile, rows), 1).astype(F32)
    g = jnp.where(slot == pos1, meta[:, 4:5], jnp.where(slot == pos2, meta[:, 5:6], 0.0))

    _segment_dma(start_ref, len_ref, i, y_ref, ybuf, sem, to_hbm=False, wait=True)
    y = _dot(g.astype(BF16), ybuf[...])
    ms = jnp.mean(y * y, axis=-1, keepdims=True)
    o_ref[...] = x_ref[...] + gt_ref[0] * (y * lax.rsqrt(ms + EPS) * gp_ref[...])


def _combine(seg_start, seg_len, meta, y, x2, g_post, mod3, local_rows, *, seq, tile):
    n, d = x2.shape
    tpr = seq // tile
    grid_spec = pltpu.PrefetchScalarGridSpec(
        num_scalar_prefetch=2,
        grid=(n // tile,),
        in_specs=[pl.BlockSpec((tile, LANES), lambda i, st, ln: (i, 0)),
                  pl.BlockSpec(memory_space=pl.ANY),
                  pl.BlockSpec((tile, d), lambda i, st, ln: (i, 0)),
                  pl.BlockSpec((1, d), lambda i, st, ln: (0, 0)),
                  pl.BlockSpec((1, 1, d), lambda i, st, ln: (i // tpr, 0, 5))],
        out_specs=pl.BlockSpec((tile, d), lambda i, st, ln: (i, 0)),
        scratch_shapes=[pltpu.VMEM((local_rows, d), BF16), pltpu.SemaphoreType.DMA(())],
    )
    return pl.pallas_call(
        _combine_kernel,
        grid_spec=grid_spec,
        out_shape=jax.ShapeDtypeStruct((n, d), F32),
        compiler_params=_cparams(("arbitrary",)),
        name="moe_combine",
    )(seg_start, seg_len, meta, y, x2, g_post, mod3)


def _round_up(v, m):
    return -(-v // m) * m


def _moe(x2, g_pre, g_post, mod3, router_pad, w1, w3, w2, layer, *, seq, tile, tf):
    n, d = x2.shape
    nt = n // tile
    i32 = jnp.int32
    h, meta, cnt = _route(x2, g_pre, mod3, router_pad, seq=seq, tile=tile)

    cnt = cnt[:, 0, :N_EXPERTS].astype(i32)
    seg_len = _round_up(cnt, SEG_ALIGN)
    seg_cum = jnp.cumsum(seg_len, axis=0)
    padded = _round_up(seg_cum[-1], SLOT_TILE)
    region_end = jnp.cumsum(padded)
    seg_start = (region_end - padded)[None, :] + seg_cum - seg_len
    loc = jnp.cumsum(seg_len, axis=1) - seg_len
    local_rows = _round_up(TOP_K * tile + N_EXPERTS * (SEG_ALIGN - 1), LANES)
    n_slots = _round_up(TOP_K * n + N_EXPERTS * (nt * (SEG_ALIGN - 1) + SLOT_TILE - 1), SLOT_TILE)
    n_slot_tiles = n_slots // SLOT_TILE
    n_used = (region_end[-1] // SLOT_TILE).astype(i32)
    tile_first = jnp.minimum(jnp.arange(n_slot_tiles, dtype=i32), n_used - 1) * SLOT_TILE
    tile_expert = jnp.sum((region_end[None, :] <= tile_first[:, None]).astype(i32), axis=1)

    e1 = meta[:, 0].astype(i32).reshape(nt, tile)
    e2 = meta[:, 1].astype(i32).reshape(nt, tile)
    lpos1 = jnp.take_along_axis(loc, e1, axis=1) + meta[:, 2].astype(i32).reshape(nt, tile)
    lpos2 = jnp.take_along_axis(loc, e2, axis=1) + meta[:, 3].astype(i32).reshape(nt, tile)

    seg_start = seg_start.reshape(-1)
    seg_len = seg_len.reshape(-1)
    hs = _dispatch(seg_start, seg_len, lpos1.reshape(nt, 1, tile), lpos2.reshape(nt, 1, tile), h,
                   n_slots, local_rows)
    y = _group(tile_expert, n_used.reshape(1), hs, w1, w3, w2, layer, tf=tf)
    return _combine(seg_start, seg_len, meta, y, x2, g_post, mod3, local_rows, seq=seq, tile=tile)


def _blockdiag_tiles(w):
    nb = w.shape[0]
    per = LANES // QKV_BLOCK
    wt = w.reshape(nb // per, per, QKV_BLOCK, QKV_BLOCK)
    eye = jnp.eye(per, dtype=w.dtype)
    t = jnp.einsum("gnio,nm->gnimo", wt, eye)
    return t.reshape(nb // per, LANES, LANES).astype(BF16)


def _lower_bounds(lb_raw):
    p = jax.nn.softmax(lb_raw.astype(F32), axis=0)
    return jnp.cumsum(p, axis=0) - p[0:1]


def kernel(x, c, w_ada, b_ada, g_pre_mix, g_post_mix, g_pre_ffn, g_post_ffn, w_in, hgrn_lb, hgrn_gnorm, mlstm_conv_w, mlstm_conv_b, mlstm_wq, mlstm_wk, mlstm_wv, mlstm_w_ig, mlstm_b_ig, mlstm_w_fg, mlstm_b_fg, mlstm_gnorm, mlstm_skip, w_proj_a, w_proj_b, w_out, ffn_w1, ffn_w3, ffn_w2, moe_router, moe_w1, moe_w3, moe_w2):
    batch, seq, d = x.shape
    depth = w_in.shape[0]
    n = batch * seq
    tile = min(512, seq)
    hg_tb = min(256, seq)
    ml_l = min(256, seq)

    bp = -(-batch // 16) * 16
    c_pad = jnp.pad(c, ((0, bp - batch), (0, 0)))
    mod = _ada(c_pad, w_ada, b_ada)
    lower = _lower_bounds(hgrn_lb)

    w_in_b = w_in.astype(BF16)
    w_pa_b, w_pb_b, w_out_b = (w.astype(BF16) for w in (w_proj_a, w_proj_b, w_out))
    ffn_b = tuple(w.astype(BF16) for w in (ffn_w1, ffn_w3, ffn_w2))
    moe_b = tuple(w.astype(BF16) for w in (moe_w1, moe_w3, moe_w2))

    x2 = x.reshape(n, d)
    row = lambda a: a.reshape(1, -1)
    for l in range(depth):
        mod3 = mod[l].reshape(bp, 1, 6 * d)
        p, lf = _in_proj(x2, row(g_pre_mix[l]), mod3, row(lower[l]), w_in_b, l,
                         seq=seq, tile=min(IN_TILE, seq))
        ya = _hgrn(p, lf, row(hgrn_gnorm[l]), batch=batch, seq=seq, tb=hg_tb, chunk=64)
        wg = jnp.concatenate([mlstm_w_ig[l], mlstm_w_fg[l]], axis=-1)
        wg = jnp.pad(wg, ((0, 0), (0, LANES - 2 * MLSTM_HEADS))).astype(BF16)
        bg = jnp.pad(jnp.concatenate([mlstm_b_ig[l], mlstm_b_fg[l]]), (0, LANES - 2 * MLSTM_HEADS))
        yb = _mlstm(p, mlstm_conv_w[l], row(mlstm_conv_b[l]), _blockdiag_tiles(mlstm_wq[l]),
                    _blockdiag_tiles(mlstm_wk[l]), _blockdiag_tiles(mlstm_wv[l]), wg, row(bg),
                    row(mlstm_gnorm[l]), row(mlstm_skip[l]), batch=batch, seq=seq, L=ml_l)
        x2 = _mix_out(ya, yb, p, x2, w_pa_b, w_pb_b, w_out_b, row(g_post_mix[l]), mod3, l,
                      seq=seq, tile=tile)
        j = l // 2
        if l % 2 == 0:
            x2 = _ffn(x2, row(g_pre_ffn[l]), row(g_post_ffn[l]), mod3, *ffn_b, j,
                      seq=seq, tile=tile, tf=1408)
        else:
            rpad = jnp.pad(moe_router[j], ((0, 0), (0, LANES - N_EXPERTS)))
            x2 = _moe(x2, row(g_pre_ffn[l]), row(g_post_ffn[l]), mod3, rpad, *moe_b, j,
                      seq=seq, tile=tile, tf=1792)
    return x2.reshape(batch, seq, d)
```

```python
import functools

import jax
import jax.numpy as jnp
from jax import lax
from jax.experimental import pallas as pl
from jax.experimental.pallas import tpu as pltpu

F32 = jnp.float32
BF16 = jnp.bfloat16
EPS = 1e-6
NEG = -1e30
LANES = 128

D_MODEL = 1024
HGRN_HEADS = 8
HGRN_DH = 128
MLSTM_HEADS = 4
MLSTM_DH = 256
MLSTM_CONV = 4
QKV_BLOCK = 4
N_EXPERTS = 8
TOP_K = 2
N_SEG = 8
IN_TILE = 1024
IN_SUB_ROWS = 256
FFN_SUB_ROWS = 256

VMEM_LIMIT = 56 * 1024 * 1024


def _cparams(sem):
    return pltpu.CompilerParams(dimension_semantics=sem, vmem_limit_bytes=VMEM_LIMIT)


def _sigmoid(x):
    return 0.5 * jnp.tanh(0.5 * x) + 0.5


def _dot(a, b):
    return jnp.dot(a, b, preferred_element_type=F32)


def _dot_nt(a, b):
    return lax.dot_general(a, b, (((1,), (1,)), ((), ())), preferred_element_type=F32)


def _dot_tn(a, b):
    return lax.dot_general(a, b, (((0,), (0,)), ((), ())), preferred_element_type=F32)


def _split3(x):
    hi = x.astype(BF16)
    r1 = x - hi.astype(F32)
    mid = r1.astype(BF16)
    lo = (r1 - mid.astype(F32)).astype(BF16)
    return hi, mid, lo


def _chunk_tril(n, chunk):
    r = lax.broadcasted_iota(jnp.int32, (n, n), 0)
    c = lax.broadcasted_iota(jnp.int32, (n, n), 1)
    keep = (r >= c) & ((r // chunk) == (c // chunk))
    return jnp.where(keep, 1.0, 0.0).astype(BF16)


def _cumsum_rows(tri, x):
    hi, mid, lo = _split3(x)
    return _dot(tri, hi) + _dot(tri, mid) + _dot(tri, lo)


def _rms_mod(x, g, sc, sh):
    ms = jnp.mean(x * x, axis=-1, keepdims=True)
    return (x * lax.rsqrt(ms + EPS) * g) * (1.0 + sc) + sh


def _ada_kernel(c_ref, w_ref, b_ref, o_ref):
    c = c_ref[...]
    ca = (c * _sigmoid(c)).astype(BF16)
    o_ref[0] = _dot(ca, w_ref[0].astype(BF16)) + b_ref[0]


def _ada(c_pad, w_ada, b_ada):
    depth, d, six_d = w_ada.shape
    bp = c_pad.shape[0]
    tn = 1024
    return pl.pallas_call(
        _ada_kernel,
        grid=(depth, six_d // tn),
        in_specs=[
            pl.BlockSpec((bp, d), lambda l, j: (0, 0)),
            pl.BlockSpec((1, d, tn), lambda l, j: (l, 0, j)),
            pl.BlockSpec((1, 1, tn), lambda l, j: (l, 0, j)),
        ],
        out_specs=pl.BlockSpec((1, bp, tn), lambda l, j: (l, 0, j)),
        out_shape=jax.ShapeDtypeStruct((depth, bp, six_d), F32),
        compiler_params=_cparams(("parallel", "parallel")),
        name="ada_mod",
    )(c_pad, w_ada, b_ada.reshape(depth, 1, six_d))


def _in_kernel(x_ref, g_ref, sc_ref, sh_ref, lb_ref, w_ref, p_ref, lf_ref, h_scr):
    j = pl.program_id(1)

    @pl.when(j == 0)
    def _():
        h_scr[...] = _rms_mod(x_ref[...], g_ref[...], sc_ref[0], sh_ref[0]).astype(BF16)

    def run(epilogue):
        for r0 in range(0, h_scr.shape[0], IN_SUB_ROWS):
            rows = slice(r0, r0 + IN_SUB_ROWS)
            epilogue(rows, _dot(h_scr[rows, :], w_ref[...]))

    def silu(rows, acc):
        p_ref[rows, :] = (acc * _sigmoid(acc)).astype(BF16)

    def forget(rows, acc):
        lb = lb_ref[...]
        sg = _sigmoid(acc)
        lf_ref[rows, :] = jnp.log(lb + (1.0 - lb) * sg)
        p_ref[rows, :] = ((1.0 - lb) * (1.0 - sg)).astype(BF16)

    def ident(rows, acc):
        p_ref[rows, :] = acc.astype(BF16)

    def sigm(rows, acc):
        p_ref[rows, :] = _sigmoid(acc).astype(BF16)

    pl.when((j == 0) | (j == 3))(lambda: run(silu))
    pl.when(j == 1)(lambda: run(forget))
    pl.when((j == 2) | (j == 4))(lambda: run(ident))
    pl.when(j >= 5)(lambda: run(sigm))


def _in_proj(x2, g_pre, mod3, lb, w_in, layer, *, seq, tile):
    n, d = x2.shape
    tiles_per_row = seq // tile
    return pl.pallas_call(
        _in_kernel,
        grid=(n // tile, N_SEG),
        in_specs=[
            pl.BlockSpec((tile, d), lambda i, j: (i, 0)),
            pl.BlockSpec((1, d), lambda i, j: (0, 0)),
            pl.BlockSpec((1, 1, d), lambda i, j: (i // tiles_per_row, 0, 1)),
            pl.BlockSpec((1, 1, d), lambda i, j: (i // tiles_per_row, 0, 0)),
            pl.BlockSpec((1, d), lambda i, j: (0, 0)),
            pl.BlockSpec((None, d, d), lambda i, j: (layer, 0, j)),
        ],
        out_specs=[
            pl.BlockSpec((tile, d), lambda i, j: (i, j)),
            pl.BlockSpec((tile, d), lambda i, j: (i, 0)),
        ],
        out_shape=[
            jax.ShapeDtypeStruct((n, N_SEG * d), BF16),
            jax.ShapeDtypeStruct((n, d), F32),
        ],
        scratch_shapes=[pltpu.VMEM((tile, d), BF16)],
        compiler_params=_cparams(("parallel", "arbitrary")),
        name="in_proj",
    )(x2, g_pre, mod3, mod3, lb, w_in)


def _hgrn_kernel(q_ref, k_ref, v_ref, sg_ref, lf_ref, gn_ref, o_ref, st_scr, *, tb, chunk):
    nseq = lf_ref.shape[0]

    @pl.when(pl.program_id(1) == 0)
    def _():
        st_scr[...] = jnp.zeros_like(st_scr)

    tri = _chunk_tril(tb, chunk)
    b_all = [_cumsum_rows(tri, lf_ref[s]) for s in range(nseq)]
    r_io = lax.broadcasted_iota(jnp.int32, (chunk, chunk), 0)
    c_io = lax.broadcasted_iota(jnp.int32, (chunk, chunk), 1)
    causal = r_io >= c_io
    mid = chunk // 2 - 1

    def head_chunk(s, rs, h):
        cs = slice(h * HGRN_DH, (h + 1) * HGRN_DH)
        b = b_all[s][rs, cs]
        b_mid = b[mid:mid + 1]
        b_last = b[chunk - 1:chunk]
        v = v_ref[s, rs, cs]
        qi = q_ref[s, rs, cs].astype(F32) * jnp.exp(b - b_mid)
        ki = k_ref[s, rs, cs].astype(F32) * jnp.exp(b_mid - b)
        q_st = (qi * jnp.exp(b_mid)).astype(BF16)
        k_st = (ki * jnp.exp(b_last - b_mid)).astype(BF16)
        sc = jnp.where(causal, _dot_nt(qi.astype(BF16), ki.astype(BF16)), 0.0).astype(BF16)
        st = st_scr[s, h]
        o = _dot(sc, v) + _dot_nt(q_st, st.astype(BF16))
        st_scr[s, h] = st * jnp.exp(b_last) + _dot_tn(v, k_st)
        ms = jnp.mean(o * o, axis=-1, keepdims=True)
        y = o * lax.rsqrt(ms + EPS) * gn_ref[:, cs] * sg_ref[s, rs, cs].astype(F32)
        o_ref[s, rs, cs] = y.astype(BF16)

    for ci in range(tb // chunk):
        rs = slice(ci * chunk, (ci + 1) * chunk)
        for h in range(HGRN_HEADS):
            for s in range(nseq):
                head_chunk(s, rs, h)


def _hgrn(p, lf, gnorm, *, batch, seq, tb, chunk, nseq):
    n, d = lf.shape
    p3 = p.reshape(batch, seq, p.shape[-1])
    col = lambda c: pl.BlockSpec((nseq, tb, d), lambda b, t: (b, t, c))
    out = pl.pallas_call(
        functools.partial(_hgrn_kernel, tb=tb, chunk=chunk),
        grid=(batch // nseq, seq // tb),
        in_specs=[col(0), col(1), col(2), col(3), col(0),
                  pl.BlockSpec((1, d), lambda b, t: (0, 0))],
        out_specs=col(0),
        out_shape=jax.ShapeDtypeStruct((batch, seq, d), BF16),
        scratch_shapes=[pltpu.VMEM((nseq, HGRN_HEADS, HGRN_DH, HGRN_DH), F32)],
        compiler_params=_cparams(("parallel", "arbitrary")),
        name="hgrn2",
    )(p3, p3, p3, p3, lf.reshape(batch, seq, d), gnorm)
    return out.reshape(n, d)


def _mlstm_kernel(xm_ref, so_ref, cw_ref, cb_ref, bdq_ref, bdk_ref, bdv_ref, wg_ref, bg_ref,
                  gn_ref, sk_ref, o_ref, xbuf, c_scr, n_scr, m_scr, *, L):
    d = D_MODEL

    @pl.when(pl.program_id(1) == 0)
    def _():
        xbuf[0:8, :] = jnp.zeros((8, d), F32)
        c_scr[...] = jnp.zeros_like(c_scr)
        n_scr[...] = jnp.zeros_like(n_scr)
        m_scr[...] = jnp.zeros_like(m_scr)

    xmb = xm_ref[...]
    xm = xmb.astype(F32)
    xbuf[8:8 + L, :] = xm
    cw = cw_ref[...]
    conv = (cb_ref[...] + cw[3:4] * xm + cw[2:3] * xbuf[7:7 + L, :]
            + cw[1:2] * xbuf[6:6 + L, :] + cw[0:1] * xbuf[5:5 + L, :])
    xbuf[0:8, :] = xm[L - 8:L]
    xc = conv * _sigmoid(conv)
    xcb = xc.astype(BF16)

    qs, ks, vs = [], [], []
    for g in range(d // LANES):
        gs = slice(g * LANES, (g + 1) * LANES)
        qs.append(_dot(xcb[:, gs], bdq_ref[g]))
        ks.append(_dot(xcb[:, gs], bdk_ref[g]))
        vs.append(_dot(xmb[:, gs], bdv_ref[g]))
    q = jnp.concatenate(qs, axis=-1)
    k = jnp.concatenate(ks, axis=-1)
    v = jnp.concatenate(vs, axis=-1)
    qb, kb, vb = q.astype(BF16), k.astype(BF16), v.astype(BF16)

    gates = (_dot(qb, wg_ref[0:d, :]) + _dot(kb, wg_ref[d:2 * d, :])
             + _dot(vb, wg_ref[2 * d:3 * d, :]) + bg_ref[...])
    lf = jnp.minimum(gates, 0.0) - jnp.log(1.0 + jnp.exp(-jnp.abs(gates)))
    bcs = _cumsum_rows(_chunk_tril(L, L), lf)
    gates_t = gates.T
    bcs_t = bcs.T

    r_io = lax.broadcasted_iota(jnp.int32, (L, L), 0)
    c_io = lax.broadcasted_iota(jnp.int32, (L, L), 1)
    causal = r_io >= c_io
    kscale = MLSTM_DH ** -0.5
    for h in range(MLSTM_HEADS):
        cs = slice(h * MLSTM_DH, (h + 1) * MLSTM_DH)
        a_row = gates_t[h:h + 1, :] - bcs_t[4 + h:5 + h, :]
        b_col = bcs[:, 4 + h:5 + h]
        a_col = gates[:, h:h + 1] - b_col
        m_prev = m_scr[h][:, 0:1]
        am = jnp.where(causal, a_row, NEG)
        m_run = jnp.maximum(jnp.max(am, axis=-1, keepdims=True), m_prev)
        e = jnp.exp(am - m_run)
        qh = q[:, cs]
        khs = k[:, cs] * kscale
        s = _dot_nt(qb[:, cs], khs.astype(BF16)) * e
        w_inter = jnp.exp(m_prev - m_run)
        num = _dot(s.astype(BF16), vb[:, cs]) + w_inter * _dot(qb[:, cs], c_scr[h].astype(BF16))
        den = (jnp.sum(s, axis=-1, keepdims=True)
               + w_inter * jnp.sum(qh * n_scr[h], axis=-1, keepdims=True))
        m_t = b_col + m_run
        hh = num / jnp.maximum(jnp.abs(den), jnp.exp(-m_t))
        m_last = m_run[L - 1:L]
        w_s = jnp.exp(a_col - m_last)
        dec = jnp.exp(m_prev - m_last)
        kw = khs * w_s
        c_scr[h] = dec * c_scr[h] + _dot_tn(kw.astype(BF16), vb[:, cs])
        n_scr[h] = dec * n_scr[h] + jnp.sum(kw, axis=0, keepdims=True)
        m_scr[h] = jnp.broadcast_to(b_col[L - 1:L] + m_last, (1, LANES))
        mu = jnp.mean(hh, axis=-1, keepdims=True)
        hc = hh - mu
        var = jnp.mean(hc * hc, axis=-1, keepdims=True)
        y = hc * lax.rsqrt(var + EPS) * gn_ref[:, cs] + sk_ref[:, cs] * xc[:, cs]
        o_ref[:, cs] = (so_ref[:, cs].astype(F32) * y).astype(BF16)


def _mlstm(p, conv_w, conv_b, bdq, bdk, bdv, wg, bg, gnorm, skip, *, batch, seq, L):
    n = p.shape[0]
    d = D_MODEL
    nt = seq // L
    col = lambda c: pl.BlockSpec((L, d), lambda b, t: (b * nt + t, c))
    full = lambda shp: pl.BlockSpec(shp, lambda b, t: (0,) * len(shp))
    return pl.pallas_call(
        functools.partial(_mlstm_kernel, L=L),
        grid=(batch, nt),
        in_specs=[col(4), col(5), full((MLSTM_CONV, d)), full((1, d)),
                  full(bdq.shape), full(bdk.shape), full(bdv.shape),
                  full(wg.shape), full((1, LANES)), full((1, d)), full((1, d))],
        out_specs=col(0),
        out_shape=jax.ShapeDtypeStruct((n, d), BF16),
        scratch_shapes=[
            pltpu.VMEM((8 + L, d), F32),
            pltpu.VMEM((MLSTM_HEADS, MLSTM_DH, MLSTM_DH), F32),
            pltpu.VMEM((MLSTM_HEADS, 1, MLSTM_DH), F32),
            pltpu.VMEM((MLSTM_HEADS, 1, LANES), F32),
        ],
        compiler_params=_cparams(("parallel", "arbitrary")),
        name="mlstm",
    )(p, p, conv_w, conv_b, bdq, bdk, bdv, wg, bg, gnorm, skip)


def _mix_out_kernel(ya_ref, yb_ref, ga_ref, gb_ref, x_ref, wa_ref, wb_ref, wo_ref, gp_ref, gt_ref,
                    o_ref):
    pa = _dot(ya_ref[...], wa_ref[...])
    pb = _dot(yb_ref[...], wb_ref[...])
    mixed = ga_ref[...].astype(F32) * pa + gb_ref[...].astype(F32) * pb
    y = _dot(mixed.astype(BF16), wo_ref[...])
    ms = jnp.mean(y * y, axis=-1, keepdims=True)
    o_ref[...] = x_ref[...] + gt_ref[0] * (y * lax.rsqrt(ms + EPS) * gp_ref[...])


def _mix_out(ya, yb, p, x2, wa, wb, wo, g_post, mod3, layer, *, seq, tile):
    n, d = x2.shape
    tpr = seq // tile
    row = lambda c: pl.BlockSpec((tile, d), lambda i: (i, c))
    wspec = pl.BlockSpec((None, d, d), lambda i: (layer, 0, 0))
    return pl.pallas_call(
        _mix_out_kernel,
        grid=(n // tile,),
        in_specs=[row(0), row(0), row(6), row(7), row(0), wspec, wspec, wspec,
                  pl.BlockSpec((1, d), lambda i: (0, 0)),
                  pl.BlockSpec((1, 1, d), lambda i: (i // tpr, 0, 2))],
        out_specs=row(0),
        out_shape=jax.ShapeDtypeStruct((n, d), F32),
        compiler_params=_cparams(("parallel",)),
        name="mix_out",
    )(ya, yb, p, p, x2, wa, wb, wo, g_post, mod3)


def _ffn_kernel(x_ref, g_ref, sc_ref, sh_ref, gp_ref, gt_ref, w1_ref, w3_ref, w2_ref, o_ref):
    for r0 in range(0, x_ref.shape[0], FFN_SUB_ROWS):
        rows = slice(r0, r0 + FFN_SUB_ROWS)
        x = x_ref[rows, :]
        h = _rms_mod(x, g_ref[...], sc_ref[0], sh_ref[0]).astype(BF16)
        a = _dot(h, w1_ref[...])
        act = (a * _sigmoid(a)) * _dot(h, w3_ref[...])
        y = _dot(act.astype(BF16), w2_ref[...])
        ms = jnp.mean(y * y, axis=-1, keepdims=True)
        o_ref[rows, :] = x + gt_ref[0] * (y * lax.rsqrt(ms + EPS) * gp_ref[...])


def _ffn(x2, g_pre, g_post, mod3, w1, w3, w2, layer, *, seq, tile):
    n, d = x2.shape
    dff = w1.shape[-1]
    tpr = seq // tile
    vec = pl.BlockSpec((1, d), lambda i: (0, 0))
    modspec = lambda c: pl.BlockSpec((1, 1, d), lambda i: (i // tpr, 0, c))
    resident = dict(pipeline_mode=pl.Buffered(1))
    return pl.pallas_call(
        _ffn_kernel,
        grid=(n // tile,),
        in_specs=[pl.BlockSpec((tile, d), lambda i: (i, 0)), vec, modspec(4), modspec(3), vec,
                  modspec(5),
                  pl.BlockSpec((None, d, dff), lambda i: (layer, 0, 0), **resident),
                  pl.BlockSpec((None, d, dff), lambda i: (layer, 0, 0), **resident),
                  pl.BlockSpec((None, dff, d), lambda i: (layer, 0, 0), **resident)],
        out_specs=pl.BlockSpec((tile, d), lambda i: (i, 0)),
        out_shape=jax.ShapeDtypeStruct((n, d), F32),
        compiler_params=_cparams(("parallel",)),
        name="ffn",
    )(x2, g_pre, mod3, mod3, g_post, mod3, w1, w3, w2)


SLOT_TILE = 512
SEG_ALIGN = 16


def _route_kernel(x_ref, g_ref, sc_ref, sh_ref, rt_ref, h_ref, meta_ref, cnt_ref):
    tile = x_ref.shape[0]
    lane = lax.broadcasted_iota(jnp.int32, (tile, LANES), 1)
    h = _rms_mod(x_ref[...], g_ref[...], sc_ref[0], sh_ref[0])
    h_hi = h.astype(BF16)
    h_ref[...] = h_hi
    h_lo = (h - h_hi.astype(F32)).astype(BF16)
    r = rt_ref[...]
    r_hi = r.astype(BF16)
    r_lo = (r - r_hi.astype(F32)).astype(BF16)
    logits = _dot(h_hi, r_hi) + _dot(h_hi, r_lo) + _dot(h_lo, r_hi)
    logits = jnp.where(lane < N_EXPERTS, logits, NEG)
    lane_f = lane.astype(F32)
    v1 = jnp.max(logits, axis=-1, keepdims=True)
    i1 = jnp.min(jnp.where(logits == v1, lane_f, float(LANES)), axis=-1, keepdims=True)
    m1 = lane_f == i1
    l2 = jnp.where(m1, NEG, logits)
    v2 = jnp.max(l2, axis=-1, keepdims=True)
    i2 = jnp.min(jnp.where(l2 == v2, lane_f, float(LANES)), axis=-1, keepdims=True)
    m2 = lane_f == i2
    e2 = jnp.exp(v2 - v1)
    inv = 1.0 / (1.0 + e2)
    onehot = jnp.where(m1, 1.0, jnp.where(m2, 1.0, 0.0))
    r_io = lax.broadcasted_iota(jnp.int32, (tile, tile), 0)
    c_io = lax.broadcasted_iota(jnp.int32, (tile, tile), 1)
    before = jnp.where(r_io > c_io, 1.0, 0.0).astype(BF16)
    rank = _dot(before, onehot.astype(BF16))
    r1 = jnp.sum(jnp.where(m1, rank, 0.0), axis=-1, keepdims=True)
    r2 = jnp.sum(jnp.where(m2, rank, 0.0), axis=-1, keepdims=True)
    cols = (i1, i2, r1, r2, inv, e2 * inv)
    meta = jnp.zeros((tile, LANES), F32)
    for idx, val in enumerate(cols):
        meta = jnp.where(lane == idx, val, meta)
    meta_ref[...] = meta
    cnt_ref[0] = jnp.sum(onehot, axis=0, keepdims=True)


def _route(x2, g_pre, mod3, router_pad, *, seq, tile):
    n, d = x2.shape
    tpr = seq // tile
    nt = n // tile
    vec = pl.BlockSpec((1, d), lambda i: (0, 0))
    modspec = lambda c: pl.BlockSpec((1, 1, d), lambda i: (i // tpr, 0, c))
    return pl.pallas_call(
        _route_kernel,
        grid=(nt,),
        in_specs=[pl.BlockSpec((tile, d), lambda i: (i, 0)), vec, modspec(4), modspec(3),
                  pl.BlockSpec((d, LANES), lambda i: (0, 0))],
        out_specs=[pl.BlockSpec((tile, d), lambda i: (i, 0)),
                   pl.BlockSpec((tile, LANES), lambda i: (i, 0)),
                   pl.BlockSpec((1, 1, LANES), lambda i: (i, 0, 0))],
        out_shape=[jax.ShapeDtypeStruct((n, d), BF16),
                   jax.ShapeDtypeStruct((n, LANES), F32),
                   jax.ShapeDtypeStruct((nt, 1, LANES), F32)],
        compiler_params=_cparams(("parallel",)),
        name="moe_route",
    )(x2, g_pre, mod3, mod3, router_pad)


def _segment_dma(start_ref, len_ref, i, hbm_ref, buf_ref, sem, *, to_hbm, wait):
    loc = jnp.int32(0)
    for e in range(N_EXPERTS):
        base = start_ref[i * N_EXPERTS + e]
        length = len_ref[i * N_EXPERTS + e]

        def chunk(c, carry, base=base, loc=loc):
            hb = hbm_ref.at[pl.ds(pl.multiple_of(base + c * SEG_ALIGN, SEG_ALIGN), SEG_ALIGN)]
            vb = buf_ref.at[pl.ds(pl.multiple_of(loc + c * SEG_ALIGN, SEG_ALIGN), SEG_ALIGN)]
            cp = pltpu.make_async_copy(vb, hb, sem) if to_hbm else pltpu.make_async_copy(hb, vb, sem)
            if wait:
                cp.wait()
            else:
                cp.start()
            return carry

        lax.fori_loop(0, length // SEG_ALIGN, chunk, 0)
        loc = loc + length


def _dispatch_kernel(start_ref, len_ref, p1_ref, p2_ref, h_ref, init_ref, hs_ref, buf, sem):
    del init_ref
    i = pl.program_id(0)
    rows = buf.shape[0]
    tk = h_ref.shape[0]
    slot = lax.broadcasted_iota(jnp.int32, (rows, tk), 0)
    sel = jnp.where(slot == p1_ref[0], 1.0, jnp.where(slot == p2_ref[0], 1.0, 0.0)).astype(BF16)
    buf[...] = _dot(sel, h_ref[...]).astype(BF16)
    _segment_dma(start_ref, len_ref, i, hs_ref, buf, sem, to_hbm=True, wait=False)
    _segment_dma(start_ref, len_ref, i, hs_ref, buf, sem, to_hbm=True, wait=True)


def _dispatch(seg_start, seg_len, lpos1_rows, lpos2_rows, h, n_slots, local_rows):
    n, d = h.shape
    tk = lpos1_rows.shape[-1]
    hs_init = jnp.zeros((n_slots, d), BF16)
    grid_spec = pltpu.PrefetchScalarGridSpec(
        num_scalar_prefetch=2,
        grid=(n // tk,),
        in_specs=[pl.BlockSpec((1, 1, tk), lambda i, st, ln: (i, 0, 0)),
                  pl.BlockSpec((1, 1, tk), lambda i, st, ln: (i, 0, 0)),
                  pl.BlockSpec((tk, d), lambda i, st, ln: (i, 0)),
                  pl.BlockSpec(memory_space=pl.ANY)],
        out_specs=pl.BlockSpec(memory_space=pl.ANY),
        scratch_shapes=[pltpu.VMEM((local_rows, d), BF16), pltpu.SemaphoreType.DMA(())],
    )
    return pl.pallas_call(
        _dispatch_kernel,
        grid_spec=grid_spec,
        out_shape=jax.ShapeDtypeStruct((n_slots, d), BF16),
        input_output_aliases={5: 0},
        compiler_params=_cparams(("arbitrary",)),
        name="moe_dispatch",
    )(seg_start, seg_len, lpos1_rows, lpos2_rows, h, hs_init)


def _group_kernel(te_ref, nu_ref, hs_ref, w1_ref, w3_ref, w2_ref, y_ref, acc_scr):
    m = pl.program_id(0)
    j = pl.program_id(1)
    last = pl.num_programs(1) - 1
    used = m < nu_ref[0]

    @pl.when(used)
    def _():
        @pl.when(j == 0)
        def _():
            acc_scr[...] = jnp.zeros_like(acc_scr)

        for r0 in range(0, hs_ref.shape[0], FFN_SUB_ROWS):
            rows = slice(r0, r0 + FFN_SUB_ROWS)
            hs = hs_ref[rows, :]
            a = _dot(hs, w1_ref[0])
            act = (a * _sigmoid(a)) * _dot(hs, w3_ref[0])
            acc_scr[rows, :] += _dot(act.astype(BF16), w2_ref[0])

        @pl.when(j == last)
        def _():
            y_ref[...] = acc_scr[...].astype(BF16)

    @pl.when(jnp.logical_not(used) & (j == last))
    def _():
        y_ref[...] = jnp.zeros_like(y_ref)


def _group(tile_expert, n_used, hs, w1, w3, w2, layer, *, tf):
    ns, d = hs.shape
    dff = w1.shape[-1]
    nj = dff // tf

    def wj(m, j, nu):
        return jnp.where(m < nu[0], j, nj - 1)

    grid_spec = pltpu.PrefetchScalarGridSpec(
        num_scalar_prefetch=2,
        grid=(ns // SLOT_TILE, nj),
        in_specs=[pl.BlockSpec((SLOT_TILE, d), lambda m, j, te, nu: (m, 0)),
                  pl.BlockSpec((None, 1, d, tf),
                               lambda m, j, te, nu: (layer, te[m], 0, wj(m, j, nu))),
                  pl.BlockSpec((None, 1, d, tf),
                               lambda m, j, te, nu: (layer, te[m], 0, wj(m, j, nu))),
                  pl.BlockSpec((None, 1, tf, d),
                               lambda m, j, te, nu: (layer, te[m], wj(m, j, nu), 0))],
        out_specs=pl.BlockSpec((SLOT_TILE, d), lambda m, j, te, nu: (m, 0)),
        scratch_shapes=[pltpu.VMEM((SLOT_TILE, d), F32)],
    )
    return pl.pallas_call(
        _group_kernel,
        grid_spec=grid_spec,
        out_shape=jax.ShapeDtypeStruct((ns, d), BF16),
        compiler_params=_cparams(("arbitrary", "arbitrary")),
        name="moe_group",
    )(tile_expert, n_used, hs, w1, w3, w2)


def _combine_kernel(start_ref, len_ref, meta_ref, y_ref, x_ref, gp_ref, gt_ref, o_ref, ybuf, sem):
    i = pl.program_id(0)
    tile = meta_ref.shape[0]
    rows = ybuf.shape[0]

    @pl.when(i == 0)
    def _():
        ybuf[...] = jnp.zeros_like(ybuf)

    _segment_dma(start_ref, len_ref, i, y_ref, ybuf, sem, to_hbm=False, wait=False)

    lane1 = lax.broadcasted_iota(jnp.int32, (1, LANES), 1)
    loc_row = jnp.zeros((1, LANES), F32)
    loc = jnp.int32(0)
    for e in range(N_EXPERTS):
        loc_row = jnp.where(lane1 == e, loc.astype(F32), loc_row)
        loc = loc + len_ref[i * N_EXPERTS + e]
    meta = meta_ref[...]
    lane_f = lax.broadcasted_iota(jnp.int32, (tile, LANES), 1).astype(F32)
    pos1 = meta[:, 2:3] + jnp.sum(jnp.where(lane_f == meta[:, 0:1], loc_row, 0.0), axis=-1,
                                  keepdims=True)
    pos2 = meta[:, 3:4] + jnp.sum(jnp.where(lane_f == meta[:, 1:2], loc_row, 0.0), axis=-1,
                                  keepdims=True)
    slot = lax.broadcasted_iota(jnp.int32, (tile, rows), 1).astype(F32)
    g = jnp.where(slot == pos1, meta[:, 4:5], jnp.where(slot == pos2, meta[:, 5:6], 0.0))

    _segment_dma(start_ref, len_ref, i, y_ref, ybuf, sem, to_hbm=False, wait=True)
    y = _dot(g.astype(BF16), ybuf[...])
    ms = jnp.mean(y * y, axis=-1, keepdims=True)
    o_ref[...] = x_ref[...] + gt_ref[0] * (y * lax.rsqrt(ms + EPS) * gp_ref[...])


def _combine(seg_start, seg_len, meta, y, x2, g_post, mod3, local_rows, *, seq, tile):
    n, d = x2.shape
    tpr = seq // tile
    grid_spec = pltpu.PrefetchScalarGridSpec(
        num_scalar_prefetch=2,
        grid=(n // tile,),
        in_specs=[pl.BlockSpec((tile, LANES), lambda i, st, ln: (i, 0)),
                  pl.BlockSpec(memory_space=pl.ANY),
                  pl.BlockSpec((tile, d), lambda i, st, ln: (i, 0)),
                  pl.BlockSpec((1, d), lambda i, st, ln: (0, 0)),
                  pl.BlockSpec((1, 1, d), lambda i, st, ln: (i // tpr, 0, 5))],
        out_specs=pl.BlockSpec((tile, d), lambda i, st, ln: (i, 0)),
        scratch_shapes=[pltpu.VMEM((local_rows, d), BF16), pltpu.SemaphoreType.DMA(())],
    )
    return pl.pallas_call(
        _combine_kernel,
        grid_spec=grid_spec,
        out_shape=jax.ShapeDtypeStruct((n, d), F32),
        compiler_params=_cparams(("arbitrary",)),
        name="moe_combine",
    )(seg_start, seg_len, meta, y, x2, g_post, mod3)


def _round_up(v, m):
    return -(-v // m) * m


def _moe(x2, g_pre, g_post, mod3, router_pad, w1, w3, w2, layer, *, seq, tile, tf):
    n, d = x2.shape
    nt = n // tile
    i32 = jnp.int32
    h, meta, cnt = _route(x2, g_pre, mod3, router_pad, seq=seq, tile=tile)

    cnt = cnt[:, 0, :N_EXPERTS].astype(i32)
    seg_len = _round_up(cnt, SEG_ALIGN)
    seg_cum = jnp.cumsum(seg_len, axis=0)
    padded = _round_up(seg_cum[-1], SLOT_TILE)
    region_end = jnp.cumsum(padded)
    seg_start = (region_end - padded)[None, :] + seg_cum - seg_len
    loc = jnp.cumsum(seg_len, axis=1) - seg_len
    local_rows = _round_up(TOP_K * tile + N_EXPERTS * (SEG_ALIGN - 1), LANES)
    n_slots = _round_up(TOP_K * n + N_EXPERTS * (nt * (SEG_ALIGN - 1) + SLOT_TILE - 1), SLOT_TILE)
    n_slot_tiles = n_slots // SLOT_TILE
    n_used = (region_end[-1] // SLOT_TILE).astype(i32)
    tile_first = jnp.minimum(jnp.arange(n_slot_tiles, dtype=i32), n_used - 1) * SLOT_TILE
    tile_expert = jnp.sum((region_end[None, :] <= tile_first[:, None]).astype(i32), axis=1)

    def local_pos(expert, rank):
        expert = expert.astype(i32).reshape(nt, tile)
        base = sum(jnp.where(expert == e, loc[:, e:e + 1], 0) for e in range(N_EXPERTS))
        return base + rank.astype(i32).reshape(nt, tile)

    lpos1 = local_pos(meta[:, 0], meta[:, 2])
    lpos2 = local_pos(meta[:, 1], meta[:, 3])

    seg_start = seg_start.reshape(-1)
    seg_len = seg_len.reshape(-1)
    hs = _dispatch(seg_start, seg_len, lpos1.reshape(nt, 1, tile), lpos2.reshape(nt, 1, tile), h,
                   n_slots, local_rows)
    y = _group(tile_expert, n_used.reshape(1), hs, w1, w3, w2, layer, tf=tf)
    return _combine(seg_start, seg_len, meta, y, x2, g_post, mod3, local_rows, seq=seq, tile=tile)


def _blockdiag_tiles(w):
    nb = w.shape[0]
    per = LANES // QKV_BLOCK
    wt = w.reshape(nb // per, per, QKV_BLOCK, QKV_BLOCK)
    eye = jnp.eye(per, dtype=w.dtype)
    t = jnp.einsum("gnio,nm->gnimo", wt, eye)
    return t.reshape(nb // per, LANES, LANES).astype(BF16)


def _lower_bounds(lb_raw):
    p = jax.nn.softmax(lb_raw.astype(F32), axis=0)
    return jnp.cumsum(p, axis=0) - p[0:1]


def kernel(x, c, w_ada, b_ada, g_pre_mix, g_post_mix, g_pre_ffn, g_post_ffn, w_in, hgrn_lb, hgrn_gnorm, mlstm_conv_w, mlstm_conv_b, mlstm_wq, mlstm_wk, mlstm_wv, mlstm_w_ig, mlstm_b_ig, mlstm_w_fg, mlstm_b_fg, mlstm_gnorm, mlstm_skip, w_proj_a, w_proj_b, w_out, ffn_w1, ffn_w3, ffn_w2, moe_router, moe_w1, moe_w3, moe_w2):
    batch, seq, d = x.shape
    depth = w_in.shape[0]
    n = batch * seq
    tile = min(512, seq)
    hg_tb = min(256, seq)
    ml_l = min(256, seq)

    bp = -(-batch // 16) * 16
    c_pad = jnp.pad(c, ((0, bp - batch), (0, 0)))
    mod = _ada(c_pad, w_ada, b_ada)
    lower = _lower_bounds(hgrn_lb)

    w_in_b = w_in.astype(BF16)
    w_pa_b, w_pb_b, w_out_b = (w.astype(BF16) for w in (w_proj_a, w_proj_b, w_out))
    ffn_b = tuple(w.astype(BF16) for w in (ffn_w1, ffn_w3, ffn_w2))
    moe_b = tuple(w.astype(BF16) for w in (moe_w1, moe_w3, moe_w2))

    x2 = x.reshape(n, d)
    row = lambda a: a.reshape(1, -1)
    for l in range(depth):
        mod3 = mod[l].reshape(bp, 1, 6 * d)
        p, lf = _in_proj(x2, row(g_pre_mix[l]), mod3, row(lower[l]), w_in_b, l,
                         seq=seq, tile=min(IN_TILE, seq))
        ya = _hgrn(p, lf, row(hgrn_gnorm[l]), batch=batch, seq=seq, tb=hg_tb, chunk=64,
                   nseq=2 if batch % 2 == 0 else 1)
        wg = jnp.concatenate([mlstm_w_ig[l], mlstm_w_fg[l]], axis=-1)
        wg = jnp.pad(wg, ((0, 0), (0, LANES - 2 * MLSTM_HEADS))).astype(BF16)
        bg = jnp.pad(jnp.concatenate([mlstm_b_ig[l], mlstm_b_fg[l]]), (0, LANES - 2 * MLSTM_HEADS))
        yb = _mlstm(p, mlstm_conv_w[l], row(mlstm_conv_b[l]), _blockdiag_tiles(mlstm_wq[l]),
                    _blockdiag_tiles(mlstm_wk[l]), _blockdiag_tiles(mlstm_wv[l]), wg, row(bg),
                    row(mlstm_gnorm[l]), row(mlstm_skip[l]), batch=batch, seq=seq, L=ml_l)
        x2 = _mix_out(ya, yb, p, x2, w_pa_b, w_pb_b, w_out_b, row(g_post_mix[l]), mod3, l,
                      seq=seq, tile=tile)
        j = l // 2
        if l % 2 == 0:
            x2 = _ffn(x2, row(g_pre_ffn[l]), row(g_post_ffn[l]), mod3, *ffn_b, j,
                      seq=seq, tile=tile)
        else:
            rpad = jnp.pad(moe_router[j], ((0, 0), (0, LANES - N_EXPERTS)))
            x2 = _moe(x2, row(g_pre_ffn[l]), row(g_post_ffn[l]), mod3, rpad, *moe_b, j,
                      seq=seq, tile=tile, tf=1792)
    return x2.reshape(batch, seq, d)
```

```python
import functools

import jax
import jax.numpy as jnp
from jax import lax
from jax.experimental import pallas as pl
from jax.experimental.pallas import tpu as pltpu

F32 = jnp.float32
BF16 = jnp.bfloat16
EPS = 1e-6
NEG = -1e30
LANES = 128

D_MODEL = 1024
HGRN_HEADS = 8
HGRN_DH = 128
MLSTM_HEADS = 4
MLSTM_DH = 256
MLSTM_CONV = 4
QKV_BLOCK = 4
N_EXPERTS = 8
TOP_K = 2
N_SEG = 8
IN_TILE = 1024
IN_SUB_ROWS = 256
FFN_SUB_ROWS = 256

VMEM_LIMIT = 56 * 1024 * 1024


def _cparams(sem):
    return pltpu.CompilerParams(dimension_semantics=sem, vmem_limit_bytes=VMEM_LIMIT)


def _sigmoid(x):
    return 0.5 * jnp.tanh(0.5 * x) + 0.5


def _dot(a, b):
    return jnp.dot(a, b, preferred_element_type=F32)


def _dot_nt(a, b):
    return lax.dot_general(a, b, (((1,), (1,)), ((), ())), preferred_element_type=F32)


def _dot_tn(a, b):
    return lax.dot_general(a, b, (((0,), (0,)), ((), ())), preferred_element_type=F32)


def _split3(x):
    hi = x.astype(BF16)
    r1 = x - hi.astype(F32)
    mid = r1.astype(BF16)
    lo = (r1 - mid.astype(F32)).astype(BF16)
    return hi, mid, lo


def _chunk_tril(n, chunk):
    r = lax.broadcasted_iota(jnp.int32, (n, n), 0)
    c = lax.broadcasted_iota(jnp.int32, (n, n), 1)
    keep = (r >= c) & ((r // chunk) == (c // chunk))
    return jnp.where(keep, 1.0, 0.0).astype(BF16)


def _cumsum_rows(tri, x):
    hi, mid, lo = _split3(x)
    return _dot(tri, hi) + _dot(tri, mid) + _dot(tri, lo)


def _rms_mod(x, g, sc, sh):
    ms = jnp.mean(x * x, axis=-1, keepdims=True)
    return (x * lax.rsqrt(ms + EPS) * g) * (1.0 + sc) + sh


def _ada_kernel(c_ref, w_ref, b_ref, o_ref):
    c = c_ref[...]
    ca = (c * _sigmoid(c)).astype(BF16)
    o_ref[0] = _dot(ca, w_ref[0].astype(BF16)) + b_ref[0]


def _ada(c_pad, w_ada, b_ada):
    depth, d, six_d = w_ada.shape
    bp = c_pad.shape[0]
    tn = 1024
    return pl.pallas_call(
        _ada_kernel,
        grid=(depth, six_d // tn),
        in_specs=[
            pl.BlockSpec((bp, d), lambda l, j: (0, 0)),
            pl.BlockSpec((1, d, tn), lambda l, j: (l, 0, j)),
            pl.BlockSpec((1, 1, tn), lambda l, j: (l, 0, j)),
        ],
        out_specs=pl.BlockSpec((1, bp, tn), lambda l, j: (l, 0, j)),
        out_shape=jax.ShapeDtypeStruct((depth, bp, six_d), F32),
        compiler_params=_cparams(("parallel", "parallel")),
        name="ada_mod",
    )(c_pad, w_ada, b_ada.reshape(depth, 1, six_d))


def _in_kernel(x_ref, g_ref, sc_ref, sh_ref, lb_ref, w_ref, p_ref, lf_ref, h_scr):
    j = pl.program_id(1)

    @pl.when(j == 0)
    def _():
        h_scr[...] = _rms_mod(x_ref[...], g_ref[...], sc_ref[0], sh_ref[0]).astype(BF16)

    def run(epilogue):
        for r0 in range(0, h_scr.shape[0], IN_SUB_ROWS):
            rows = slice(r0, r0 + IN_SUB_ROWS)
            epilogue(rows, _dot(h_scr[rows, :], w_ref[...]))

    def silu(rows, acc):
        p_ref[rows, :] = (acc * _sigmoid(acc)).astype(BF16)

    def forget(rows, acc):
        lb = lb_ref[...]
        sg = _sigmoid(acc)
        lf_ref[rows, :] = jnp.log(lb + (1.0 - lb) * sg)
        p_ref[rows, :] = ((1.0 - lb) * (1.0 - sg)).astype(BF16)

    def ident(rows, acc):
        p_ref[rows, :] = acc.astype(BF16)

    def sigm(rows, acc):
        p_ref[rows, :] = _sigmoid(acc).astype(BF16)

    pl.when((j == 0) | (j == 3))(lambda: run(silu))
    pl.when(j == 1)(lambda: run(forget))
    pl.when((j == 2) | (j == 4))(lambda: run(ident))
    pl.when(j >= 5)(lambda: run(sigm))


def _in_proj(x2, g_pre, mod3, lb, w_in, layer, *, seq, tile):
    n, d = x2.shape
    tiles_per_row = seq // tile
    return pl.pallas_call(
        _in_kernel,
        grid=(n // tile, N_SEG),
        in_specs=[
            pl.BlockSpec((tile, d), lambda i, j: (i, 0)),
            pl.BlockSpec((1, d), lambda i, j: (0, 0)),
            pl.BlockSpec((1, 1, d), lambda i, j: (i // tiles_per_row, 0, 1)),
            pl.BlockSpec((1, 1, d), lambda i, j: (i // tiles_per_row, 0, 0)),
            pl.BlockSpec((1, d), lambda i, j: (0, 0)),
            pl.BlockSpec((None, d, d), lambda i, j: (layer, 0, j)),
        ],
        out_specs=[
            pl.BlockSpec((tile, d), lambda i, j: (i, j)),
            pl.BlockSpec((tile, d), lambda i, j: (i, 0)),
        ],
        out_shape=[
            jax.ShapeDtypeStruct((n, N_SEG * d), BF16),
            jax.ShapeDtypeStruct((n, d), F32),
        ],
        scratch_shapes=[pltpu.VMEM((tile, d), BF16)],
        compiler_params=_cparams(("parallel", "arbitrary")),
        name="in_proj",
    )(x2, g_pre, mod3, mod3, lb, w_in)


def _hgrn_units(q_ref, k_ref, v_ref, sg_ref, lf_ref, gn_ref, o_ref, st_scr, *, tb, chunk):
    tri = _chunk_tril(tb, chunk)
    b_all = _cumsum_rows(tri, lf_ref[0])
    r_io = lax.broadcasted_iota(jnp.int32, (chunk, chunk), 0)
    c_io = lax.broadcasted_iota(jnp.int32, (chunk, chunk), 1)
    causal = r_io >= c_io
    mid = chunk // 2 - 1

    def head_chunk(rs, h):
        cs = slice(h * HGRN_DH, (h + 1) * HGRN_DH)
        b = b_all[rs, cs]
        b_mid = b[mid:mid + 1]
        b_last = b[chunk - 1:chunk]
        v = v_ref[0, rs, cs]
        qi = q_ref[0, rs, cs].astype(F32) * jnp.exp(b - b_mid)
        ki = k_ref[0, rs, cs].astype(F32) * jnp.exp(b_mid - b)
        q_st = (qi * jnp.exp(b_mid)).astype(BF16)
        k_st = (ki * jnp.exp(b_last - b_mid)).astype(BF16)
        sc = jnp.where(causal, _dot_nt(qi.astype(BF16), ki.astype(BF16)), 0.0).astype(BF16)
        st = st_scr[h]
        o = _dot(sc, v) + _dot_nt(q_st, st.astype(BF16))
        st_scr[h] = st * jnp.exp(b_last) + _dot_tn(v, k_st)
        ms = jnp.mean(o * o, axis=-1, keepdims=True)
        y = o * lax.rsqrt(ms + EPS) * gn_ref[:, cs] * sg_ref[0, rs, cs].astype(F32)
        o_ref[0, rs, cs] = y.astype(BF16)

    def chunk_unit(ci):
        rs = slice(ci * chunk, (ci + 1) * chunk)
        for h in range(HGRN_HEADS):
            head_chunk(rs, h)

    return [functools.partial(chunk_unit, ci) for ci in range(tb // chunk)]


def _mlstm_units(xm_ref, so_ref, cw_ref, cb_ref, bdq_ref, bdk_ref, bdv_ref, wg_ref, bg_ref,
                 gn_ref, sk_ref, o_ref, xbuf, c_scr, n_scr, m_scr, *, L):
    d = D_MODEL
    xmb = xm_ref[0]
    xm = xmb.astype(F32)
    xbuf[8:8 + L, :] = xm
    cw = cw_ref[...]
    conv = (cb_ref[...] + cw[3:4] * xm + cw[2:3] * xbuf[7:7 + L, :]
            + cw[1:2] * xbuf[6:6 + L, :] + cw[0:1] * xbuf[5:5 + L, :])
    xbuf[0:8, :] = xm[L - 8:L]
    xc = conv * _sigmoid(conv)
    xcb = xc.astype(BF16)

    qs, ks, vs = [], [], []
    for g in range(d // LANES):
        gs = slice(g * LANES, (g + 1) * LANES)
        qs.append(_dot(xcb[:, gs], bdq_ref[g]))
        ks.append(_dot(xcb[:, gs], bdk_ref[g]))
        vs.append(_dot(xmb[:, gs], bdv_ref[g]))
    q = jnp.concatenate(qs, axis=-1)
    k = jnp.concatenate(ks, axis=-1)
    v = jnp.concatenate(vs, axis=-1)
    qb, kb, vb = q.astype(BF16), k.astype(BF16), v.astype(BF16)

    gates = (_dot(qb, wg_ref[0:d, :]) + _dot(kb, wg_ref[d:2 * d, :])
             + _dot(vb, wg_ref[2 * d:3 * d, :]) + bg_ref[...])
    lf = jnp.minimum(gates, 0.0) - jnp.log(1.0 + jnp.exp(-jnp.abs(gates)))
    bcs = _cumsum_rows(_chunk_tril(L, L), lf)
    gates_t = gates.T
    bcs_t = bcs.T

    r_io = lax.broadcasted_iota(jnp.int32, (L, L), 0)
    c_io = lax.broadcasted_iota(jnp.int32, (L, L), 1)
    causal = r_io >= c_io
    kscale = MLSTM_DH ** -0.5

    def head(h):
        cs = slice(h * MLSTM_DH, (h + 1) * MLSTM_DH)
        a_row = gates_t[h:h + 1, :] - bcs_t[4 + h:5 + h, :]
        b_col = bcs[:, 4 + h:5 + h]
        a_col = gates[:, h:h + 1] - b_col
        m_prev = m_scr[h][:, 0:1]
        am = jnp.where(causal, a_row, NEG)
        m_run = jnp.maximum(jnp.max(am, axis=-1, keepdims=True), m_prev)
        e = jnp.exp(am - m_run)
        qh = q[:, cs]
        khs = k[:, cs] * kscale
        s = _dot_nt(qb[:, cs], khs.astype(BF16)) * e
        w_inter = jnp.exp(m_prev - m_run)
        num = _dot(s.astype(BF16), vb[:, cs]) + w_inter * _dot(qb[:, cs], c_scr[h].astype(BF16))
        den = (jnp.sum(s, axis=-1, keepdims=True)
               + w_inter * jnp.sum(qh * n_scr[h], axis=-1, keepdims=True))
        m_t = b_col + m_run
        hh = num / jnp.maximum(jnp.abs(den), jnp.exp(-m_t))
        m_last = m_run[L - 1:L]
        w_s = jnp.exp(a_col - m_last)
        dec = jnp.exp(m_prev - m_last)
        kw = khs * w_s
        c_scr[h] = dec * c_scr[h] + _dot_tn(kw.astype(BF16), vb[:, cs])
        n_scr[h] = dec * n_scr[h] + jnp.sum(kw, axis=0, keepdims=True)
        m_scr[h] = jnp.broadcast_to(b_col[L - 1:L] + m_last, (1, LANES))
        mu = jnp.mean(hh, axis=-1, keepdims=True)
        hc = hh - mu
        var = jnp.mean(hc * hc, axis=-1, keepdims=True)
        y = hc * lax.rsqrt(var + EPS) * gn_ref[:, cs] + sk_ref[:, cs] * xc[:, cs]
        o_ref[0, :, cs] = (so_ref[0, :, cs].astype(F32) * y).astype(BF16)

    return [functools.partial(head, h) for h in range(MLSTM_HEADS)]


def _mixer_kernel(q_ref, k_ref, v_ref, sg_ref, lf_ref, gnh_ref, xm_ref, so_ref, cw_ref, cb_ref,
                  bdq_ref, bdk_ref, bdv_ref, wg_ref, bg_ref, gnm_ref, sk_ref, ya_ref, yb_ref,
                  st_scr, xbuf, c_scr, n_scr, m_scr, *, tb, chunk):
    @pl.when(pl.program_id(1) == 0)
    def _():
        st_scr[...] = jnp.zeros_like(st_scr)
        xbuf[0:8, :] = jnp.zeros((8, D_MODEL), F32)
        c_scr[...] = jnp.zeros_like(c_scr)
        n_scr[...] = jnp.zeros_like(n_scr)
        m_scr[...] = jnp.zeros_like(m_scr)

    m_units = _mlstm_units(xm_ref, so_ref, cw_ref, cb_ref, bdq_ref, bdk_ref, bdv_ref, wg_ref,
                           bg_ref, gnm_ref, sk_ref, yb_ref, xbuf, c_scr, n_scr, m_scr, L=tb)
    h_units = _hgrn_units(q_ref, k_ref, v_ref, sg_ref, lf_ref, gnh_ref, ya_ref, st_scr,
                          tb=tb, chunk=chunk)
    for i in range(max(len(m_units), len(h_units))):
        if i < len(h_units):
            h_units[i]()
        if i < len(m_units):
            m_units[i]()


def _mixer(p, lf, hgrn_gnorm, conv_w, conv_b, bdq, bdk, bdv, wg, bg, mlstm_gnorm, skip, *,
           batch, seq, tb, chunk):
    n, d = lf.shape
    p3 = p.reshape(batch, seq, p.shape[-1])
    col = lambda c: pl.BlockSpec((1, tb, d), lambda b, t: (b, t, c))
    full = lambda shp: pl.BlockSpec(shp, lambda b, t: (0,) * len(shp))
    vec = full((1, d))
    ya, yb = pl.pallas_call(
        functools.partial(_mixer_kernel, tb=tb, chunk=chunk),
        grid=(batch, seq // tb),
        in_specs=[col(0), col(1), col(2), col(3), col(0), vec,
                  col(4), col(5), full((MLSTM_CONV, d)), vec,
                  full(bdq.shape), full(bdk.shape), full(bdv.shape),
                  full(wg.shape), full((1, LANES)), vec, vec],
        out_specs=[col(0), col(0)],
        out_shape=[jax.ShapeDtypeStruct((batch, seq, d), BF16)] * 2,
        scratch_shapes=[
            pltpu.VMEM((HGRN_HEADS, HGRN_DH, HGRN_DH), F32),
            pltpu.VMEM((8 + tb, d), F32),
            pltpu.VMEM((MLSTM_HEADS, MLSTM_DH, MLSTM_DH), F32),
            pltpu.VMEM((MLSTM_HEADS, 1, MLSTM_DH), F32),
            pltpu.VMEM((MLSTM_HEADS, 1, LANES), F32),
        ],
        compiler_params=_cparams(("parallel", "arbitrary")),
        name="mixer",
    )(p3, p3, p3, p3, lf.reshape(batch, seq, d), hgrn_gnorm,
      p3, p3, conv_w, conv_b, bdq, bdk, bdv, wg, bg, mlstm_gnorm, skip)
    return ya.reshape(n, d), yb.reshape(n, d)


def _mix_out_kernel(ya_ref, yb_ref, ga_ref, gb_ref, x_ref, wa_ref, wb_ref, wo_ref, gp_ref, gt_ref,
                    o_ref):
    pa = _dot(ya_ref[...], wa_ref[...])
    pb = _dot(yb_ref[...], wb_ref[...])
    mixed = ga_ref[...].astype(F32) * pa + gb_ref[...].astype(F32) * pb
    y = _dot(mixed.astype(BF16), wo_ref[...])
    ms = jnp.mean(y * y, axis=-1, keepdims=True)
    o_ref[...] = x_ref[...] + gt_ref[0] * (y * lax.rsqrt(ms + EPS) * gp_ref[...])


def _mix_out(ya, yb, p, x2, wa, wb, wo, g_post, mod3, layer, *, seq, tile):
    n, d = x2.shape
    tpr = seq // tile
    row = lambda c: pl.BlockSpec((tile, d), lambda i: (i, c))
    wspec = pl.BlockSpec((None, d, d), lambda i: (layer, 0, 0))
    return pl.pallas_call(
        _mix_out_kernel,
        grid=(n // tile,),
        in_specs=[row(0), row(0), row(6), row(7), row(0), wspec, wspec, wspec,
                  pl.BlockSpec((1, d), lambda i: (0, 0)),
                  pl.BlockSpec((1, 1, d), lambda i: (i // tpr, 0, 2))],
        out_specs=row(0),
        out_shape=jax.ShapeDtypeStruct((n, d), F32),
        compiler_params=_cparams(("parallel",)),
        name="mix_out",
    )(ya, yb, p, p, x2, wa, wb, wo, g_post, mod3)


def _ffn_kernel(x_ref, g_ref, sc_ref, sh_ref, gp_ref, gt_ref, w1_ref, w3_ref, w2_ref, o_ref):
    for r0 in range(0, x_ref.shape[0], FFN_SUB_ROWS):
        rows = slice(r0, r0 + FFN_SUB_ROWS)
        x = x_ref[rows, :]
        h = _rms_mod(x, g_ref[...], sc_ref[0], sh_ref[0]).astype(BF16)
        a = _dot(h, w1_ref[...])
        act = (a * _sigmoid(a)) * _dot(h, w3_ref[...])
        y = _dot(act.astype(BF16), w2_ref[...])
        ms = jnp.mean(y * y, axis=-1, keepdims=True)
        o_ref[rows, :] = x + gt_ref[0] * (y * lax.rsqrt(ms + EPS) * gp_ref[...])


def _ffn(x2, g_pre, g_post, mod3, w1, w3, w2, layer, *, seq, tile):
    n, d = x2.shape
    dff = w1.shape[-1]
    tpr = seq // tile
    vec = pl.BlockSpec((1, d), lambda i: (0, 0))
    modspec = lambda c: pl.BlockSpec((1, 1, d), lambda i: (i // tpr, 0, c))
    resident = dict(pipeline_mode=pl.Buffered(1))
    return pl.pallas_call(
        _ffn_kernel,
        grid=(n // tile,),
        in_specs=[pl.BlockSpec((tile, d), lambda i: (i, 0)), vec, modspec(4), modspec(3), vec,
                  modspec(5),
                  pl.BlockSpec((None, d, dff), lambda i: (layer, 0, 0), **resident),
                  pl.BlockSpec((None, d, dff), lambda i: (layer, 0, 0), **resident),
                  pl.BlockSpec((None, dff, d), lambda i: (layer, 0, 0), **resident)],
        out_specs=pl.BlockSpec((tile, d), lambda i: (i, 0)),
        out_shape=jax.ShapeDtypeStruct((n, d), F32),
        compiler_params=_cparams(("parallel",)),
        name="ffn",
    )(x2, g_pre, mod3, mod3, g_post, mod3, w1, w3, w2)


SLOT_TILE = 512
SEG_ALIGN = 16


def _route_kernel(x_ref, g_ref, sc_ref, sh_ref, rt_ref, h_ref, meta_ref, cnt_ref):
    tile = x_ref.shape[0]
    lane = lax.broadcasted_iota(jnp.int32, (tile, LANES), 1)
    h = _rms_mod(x_ref[...], g_ref[...], sc_ref[0], sh_ref[0])
    h_hi = h.astype(BF16)
    h_ref[...] = h_hi
    h_lo = (h - h_hi.astype(F32)).astype(BF16)
    r = rt_ref[...]
    r_hi = r.astype(BF16)
    r_lo = (r - r_hi.astype(F32)).astype(BF16)
    logits = _dot(h_hi, r_hi) + _dot(h_hi, r_lo) + _dot(h_lo, r_hi)
    logits = jnp.where(lane < N_EXPERTS, logits, NEG)
    lane_f = lane.astype(F32)
    v1 = jnp.max(logits, axis=-1, keepdims=True)
    i1 = jnp.min(jnp.where(logits == v1, lane_f, float(LANES)), axis=-1, keepdims=True)
    m1 = lane_f == i1
    l2 = jnp.where(m1, NEG, logits)
    v2 = jnp.max(l2, axis=-1, keepdims=True)
    i2 = jnp.min(jnp.where(l2 == v2, lane_f, float(LANES)), axis=-1, keepdims=True)
    m2 = lane_f == i2
    e2 = jnp.exp(v2 - v1)
    inv = 1.0 / (1.0 + e2)
    onehot = jnp.where(m1, 1.0, jnp.where(m2, 1.0, 0.0))
    r_io = lax.broadcasted_iota(jnp.int32, (tile, tile), 0)
    c_io = lax.broadcasted_iota(jnp.int32, (tile, tile), 1)
    before = jnp.where(r_io > c_io, 1.0, 0.0).astype(BF16)
    rank = _dot(before, onehot.astype(BF16))
    r1 = jnp.sum(jnp.where(m1, rank, 0.0), axis=-1, keepdims=True)
    r2 = jnp.sum(jnp.where(m2, rank, 0.0), axis=-1, keepdims=True)
    cols = (i1, i2, r1, r2, inv, e2 * inv)
    meta = jnp.zeros((tile, LANES), F32)
    for idx, val in enumerate(cols):
        meta = jnp.where(lane == idx, val, meta)
    meta_ref[...] = meta
    cnt_ref[0] = jnp.sum(onehot, axis=0, keepdims=True)


def _route(x2, g_pre, mod3, router_pad, *, seq, tile):
    n, d = x2.shape
    tpr = seq // tile
    nt = n // tile
    vec = pl.BlockSpec((1, d), lambda i: (0, 0))
    modspec = lambda c: pl.BlockSpec((1, 1, d), lambda i: (i // tpr, 0, c))
    return pl.pallas_call(
        _route_kernel,
        grid=(nt,),
        in_specs=[pl.BlockSpec((tile, d), lambda i: (i, 0)), vec, modspec(4), modspec(3),
                  pl.BlockSpec((d, LANES), lambda i: (0, 0))],
        out_specs=[pl.BlockSpec((tile, d), lambda i: (i, 0)),
                   pl.BlockSpec((tile, LANES), lambda i: (i, 0)),
                   pl.BlockSpec((1, 1, LANES), lambda i: (i, 0, 0))],
        out_shape=[jax.ShapeDtypeStruct((n, d), BF16),
                   jax.ShapeDtypeStruct((n, LANES), F32),
                   jax.ShapeDtypeStruct((nt, 1, LANES), F32)],
        compiler_params=_cparams(("parallel",)),
        name="moe_route",
    )(x2, g_pre, mod3, mod3, router_pad)


def _segment_dma(start_ref, len_ref, i, hbm_ref, buf_ref, sem, *, to_hbm, wait):
    loc = jnp.int32(0)
    for e in range(N_EXPERTS):
        base = start_ref[i * N_EXPERTS + e]
        length = len_ref[i * N_EXPERTS + e]

        def chunk(c, carry, base=base, loc=loc):
            hb = hbm_ref.at[pl.ds(pl.multiple_of(base + c * SEG_ALIGN, SEG_ALIGN), SEG_ALIGN)]
            vb = buf_ref.at[pl.ds(pl.multiple_of(loc + c * SEG_ALIGN, SEG_ALIGN), SEG_ALIGN)]
            cp = pltpu.make_async_copy(vb, hb, sem) if to_hbm else pltpu.make_async_copy(hb, vb, sem)
            if wait:
                cp.wait()
            else:
                cp.start()
            return carry

        lax.fori_loop(0, length // SEG_ALIGN, chunk, 0)
        loc = loc + length


def _dispatch_kernel(start_ref, len_ref, p1_ref, p2_ref, h_ref, init_ref, hs_ref, buf, sem):
    del init_ref
    i = pl.program_id(0)
    rows = buf.shape[0]
    tk = h_ref.shape[0]
    slot = lax.broadcasted_iota(jnp.int32, (rows, tk), 0)
    sel = jnp.where(slot == p1_ref[0], 1.0, jnp.where(slot == p2_ref[0], 1.0, 0.0)).astype(BF16)
    buf[...] = _dot(sel, h_ref[...]).astype(BF16)
    _segment_dma(start_ref, len_ref, i, hs_ref, buf, sem, to_hbm=True, wait=False)
    _segment_dma(start_ref, len_ref, i, hs_ref, buf, sem, to_hbm=True, wait=True)


def _dispatch(seg_start, seg_len, lpos1_rows, lpos2_rows, h, n_slots, local_rows):
    n, d = h.shape
    tk = lpos1_rows.shape[-1]
    hs_init = jnp.zeros((n_slots, d), BF16)
    grid_spec = pltpu.PrefetchScalarGridSpec(
        num_scalar_prefetch=2,
        grid=(n // tk,),
        in_specs=[pl.BlockSpec((1, 1, tk), lambda i, st, ln: (i, 0, 0)),
                  pl.BlockSpec((1, 1, tk), lambda i, st, ln: (i, 0, 0)),
                  pl.BlockSpec((tk, d), lambda i, st, ln: (i, 0)),
                  pl.BlockSpec(memory_space=pl.ANY)],
        out_specs=pl.BlockSpec(memory_space=pl.ANY),
        scratch_shapes=[pltpu.VMEM((local_rows, d), BF16), pltpu.SemaphoreType.DMA(())],
    )
    return pl.pallas_call(
        _dispatch_kernel,
        grid_spec=grid_spec,
        out_shape=jax.ShapeDtypeStruct((n_slots, d), BF16),
        input_output_aliases={5: 0},
        compiler_params=_cparams(("arbitrary",)),
        name="moe_dispatch",
    )(seg_start, seg_len, lpos1_rows, lpos2_rows, h, hs_init)


def _group_kernel(te_ref, nu_ref, hs_ref, w1_ref, w3_ref, w2_ref, y_ref, acc_scr):
    m = pl.program_id(0)
    j = pl.program_id(1)
    last = pl.num_programs(1) - 1
    used = m < nu_ref[0]

    @pl.when(used)
    def _():
        @pl.when(j == 0)
        def _():
            acc_scr[...] = jnp.zeros_like(acc_scr)

        for r0 in range(0, hs_ref.shape[0], FFN_SUB_ROWS):
            rows = slice(r0, r0 + FFN_SUB_ROWS)
            hs = hs_ref[rows, :]
            a = _dot(hs, w1_ref[0])
            act = (a * _sigmoid(a)) * _dot(hs, w3_ref[0])
            acc_scr[rows, :] += _dot(act.astype(BF16), w2_ref[0])

        @pl.when(j == last)
        def _():
            y_ref[...] = acc_scr[...].astype(BF16)

    @pl.when(jnp.logical_not(used) & (j == last))
    def _():
        y_ref[...] = jnp.zeros_like(y_ref)


def _group(tile_expert, n_used, hs, w1, w3, w2, layer, *, tf):
    ns, d = hs.shape
    dff = w1.shape[-1]
    nj = dff // tf

    def wj(m, j, nu):
        return jnp.where(m < nu[0], j, nj - 1)

    grid_spec = pltpu.PrefetchScalarGridSpec(
        num_scalar_prefetch=2,
        grid=(ns // SLOT_TILE, nj),
        in_specs=[pl.BlockSpec((SLOT_TILE, d), lambda m, j, te, nu: (m, 0)),
                  pl.BlockSpec((None, 1, d, tf),
                               lambda m, j, te, nu: (layer, te[m], 0, wj(m, j, nu))),
                  pl.BlockSpec((None, 1, d, tf),
                               lambda m, j, te, nu: (layer, te[m], 0, wj(m, j, nu))),
                  pl.BlockSpec((None, 1, tf, d),
                               lambda m, j, te, nu: (layer, te[m], wj(m, j, nu), 0))],
        out_specs=pl.BlockSpec((SLOT_TILE, d), lambda m, j, te, nu: (m, 0)),
        scratch_shapes=[pltpu.VMEM((SLOT_TILE, d), F32)],
    )
    return pl.pallas_call(
        _group_kernel,
        grid_spec=grid_spec,
        out_shape=jax.ShapeDtypeStruct((ns, d), BF16),
        compiler_params=_cparams(("arbitrary", "arbitrary")),
        name="moe_group",
    )(tile_expert, n_used, hs, w1, w3, w2)


def _combine_kernel(start_ref, len_ref, meta_ref, y_ref, x_ref, gp_ref, gt_ref, o_ref, ybuf, sem):
    i = pl.program_id(0)
    tile = meta_ref.shape[0]
    rows = ybuf.shape[0]

    @pl.when(i == 0)
    def _():
        ybuf[...] = jnp.zeros_like(ybuf)

    _segment_dma(start_ref, len_ref, i, y_ref, ybuf, sem, to_hbm=False, wait=False)

    lane1 = lax.broadcasted_iota(jnp.int32, (1, LANES), 1)
    loc_row = jnp.zeros((1, LANES), F32)
    loc = jnp.int32(0)
    for e in range(N_EXPERTS):
        loc_row = jnp.where(lane1 == e, loc.astype(F32), loc_row)
        loc = loc + len_ref[i * N_EXPERTS + e]
    meta = meta_ref[...]
    lane_f = lax.broadcasted_iota(jnp.int32, (tile, LANES), 1).astype(F32)
    pos1 = meta[:, 2:3] + jnp.sum(jnp.where(lane_f == meta[:, 0:1], loc_row, 0.0), axis=-1,
                                  keepdims=True)
    pos2 = meta[:, 3:4] + jnp.sum(jnp.where(lane_f == meta[:, 1:2], loc_row, 0.0), axis=-1,
                                  keepdims=True)
    slot = lax.broadcasted_iota(jnp.int32, (tile, rows), 1).astype(F32)
    g = jnp.where(slot == pos1, meta[:, 4:5], jnp.where(slot == pos2, meta[:, 5:6], 0.0))

    _segment_dma(start_ref, len_ref, i, y_ref, ybuf, sem, to_hbm=False, wait=True)
    y = _dot(g.astype(BF16), ybuf[...])
    ms = jnp.mean(y * y, axis=-1, keepdims=True)
    o_ref[...] = x_ref[...] + gt_ref[0] * (y * lax.rsqrt(ms + EPS) * gp_ref[...])


def _combine(seg_start, seg_len, meta, y, x2, g_post, mod3, local_rows, *, seq, tile):
    n, d = x2.shape
    tpr = seq // tile
    grid_spec = pltpu.PrefetchScalarGridSpec(
        num_scalar_prefetch=2,
        grid=(n // tile,),
        in_specs=[pl.BlockSpec((tile, LANES), lambda i, st, ln: (i, 0)),
                  pl.BlockSpec(memory_space=pl.ANY),
                  pl.BlockSpec((tile, d), lambda i, st, ln: (i, 0)),
                  pl.BlockSpec((1, d), lambda i, st, ln: (0, 0)),
                  pl.BlockSpec((1, 1, d), lambda i, st, ln: (i // tpr, 0, 5))],
        out_specs=pl.BlockSpec((tile, d), lambda i, st, ln: (i, 0)),
        scratch_shapes=[pltpu.VMEM((local_rows, d), BF16), pltpu.SemaphoreType.DMA(())],
    )
    return pl.pallas_call(
        _combine_kernel,
        grid_spec=grid_spec,
        out_shape=jax.ShapeDtypeStruct((n, d), F32),
        compiler_params=_cparams(("arbitrary",)),
        name="moe_combine",
    )(seg_start, seg_len, meta, y, x2, g_post, mod3)


def _round_up(v, m):
    return -(-v // m) * m


def _moe(x2, g_pre, g_post, mod3, router_pad, w1, w3, w2, layer, *, seq, tile, tf):
    n, d = x2.shape
    nt = n // tile
    i32 = jnp.int32
    h, meta, cnt = _route(x2, g_pre, mod3, router_pad, seq=seq, tile=tile)

    cnt = cnt[:, 0, :N_EXPERTS].astype(i32)
    seg_len = _round_up(cnt, SEG_ALIGN)
    seg_cum = jnp.cumsum(seg_len, axis=0)
    padded = _round_up(seg_cum[-1], SLOT_TILE)
    region_end = jnp.cumsum(padded)
    seg_start = (region_end - padded)[None, :] + seg_cum - seg_len
    loc = jnp.cumsum(seg_len, axis=1) - seg_len
    local_rows = _round_up(TOP_K * tile + N_EXPERTS * (SEG_ALIGN - 1), LANES)
    n_slots = _round_up(TOP_K * n + N_EXPERTS * (nt * (SEG_ALIGN - 1) + SLOT_TILE - 1), SLOT_TILE)
    n_slot_tiles = n_slots // SLOT_TILE
    n_used = (region_end[-1] // SLOT_TILE).astype(i32)
    tile_first = jnp.minimum(jnp.arange(n_slot_tiles, dtype=i32), n_used - 1) * SLOT_TILE
    tile_expert = jnp.sum((region_end[None, :] <= tile_first[:, None]).astype(i32), axis=1)

    def local_pos(expert, rank):
        expert = expert.astype(i32).reshape(nt, tile)
        base = sum(jnp.where(expert == e, loc[:, e:e + 1], 0) for e in range(N_EXPERTS))
        return base + rank.astype(i32).reshape(nt, tile)

    lpos1 = local_pos(meta[:, 0], meta[:, 2])
    lpos2 = local_pos(meta[:, 1], meta[:, 3])

    seg_start = seg_start.reshape(-1)
    seg_len = seg_len.reshape(-1)
    hs = _dispatch(seg_start, seg_len, lpos1.reshape(nt, 1, tile), lpos2.reshape(nt, 1, tile), h,
                   n_slots, local_rows)
    y = _group(tile_expert, n_used.reshape(1), hs, w1, w3, w2, layer, tf=tf)
    return _combine(seg_start, seg_len, meta, y, x2, g_post, mod3, local_rows, seq=seq, tile=tile)


def _blockdiag_tiles(w):
    nb = w.shape[0]
    per = LANES // QKV_BLOCK
    wt = w.reshape(nb // per, per, QKV_BLOCK, QKV_BLOCK)
    eye = jnp.eye(per, dtype=w.dtype)
    t = jnp.einsum("gnio,nm->gnimo", wt, eye)
    return t.reshape(nb // per, LANES, LANES).astype(BF16)


def _lower_bounds(lb_raw):
    p = jax.nn.softmax(lb_raw.astype(F32), axis=0)
    return jnp.cumsum(p, axis=0) - p[0:1]


def kernel(x, c, w_ada, b_ada, g_pre_mix, g_post_mix, g_pre_ffn, g_post_ffn, w_in, hgrn_lb, hgrn_gnorm, mlstm_conv_w, mlstm_conv_b, mlstm_wq, mlstm_wk, mlstm_wv, mlstm_w_ig, mlstm_b_ig, mlstm_w_fg, mlstm_b_fg, mlstm_gnorm, mlstm_skip, w_proj_a, w_proj_b, w_out, ffn_w1, ffn_w3, ffn_w2, moe_router, moe_w1, moe_w3, moe_w2):
    batch, seq, d = x.shape
    depth = w_in.shape[0]
    n = batch * seq
    tile = min(512, seq)
    mix_tb = min(256, seq)

    bp = -(-batch // 16) * 16
    c_pad = jnp.pad(c, ((0, bp - batch), (0, 0)))
    mod = _ada(c_pad, w_ada, b_ada)
    lower = _lower_bounds(hgrn_lb)

    w_in_b = w_in.astype(BF16)
    w_pa_b, w_pb_b, w_out_b = (w.astype(BF16) for w in (w_proj_a, w_proj_b, w_out))
    ffn_b = tuple(w.astype(BF16) for w in (ffn_w1, ffn_w3, ffn_w2))
    moe_b = tuple(w.astype(BF16) for w in (moe_w1, moe_w3, moe_w2))

    x2 = x.reshape(n, d)
    row = lambda a: a.reshape(1, -1)
    for l in range(depth):
        mod3 = mod[l].reshape(bp, 1, 6 * d)
        p, lf = _in_proj(x2, row(g_pre_mix[l]), mod3, row(lower[l]), w_in_b, l,
                         seq=seq, tile=min(IN_TILE, seq))
        wg = jnp.concatenate([mlstm_w_ig[l], mlstm_w_fg[l]], axis=-1)
        wg = jnp.pad(wg, ((0, 0), (0, LANES - 2 * MLSTM_HEADS))).astype(BF16)
        bg = jnp.pad(jnp.concatenate([mlstm_b_ig[l], mlstm_b_fg[l]]), (0, LANES - 2 * MLSTM_HEADS))
        ya, yb = _mixer(p, lf, row(hgrn_gnorm[l]), mlstm_conv_w[l], row(mlstm_conv_b[l]),
                        _blockdiag_tiles(mlstm_wq[l]), _blockdiag_tiles(mlstm_wk[l]),
                        _blockdiag_tiles(mlstm_wv[l]), wg, row(bg), row(mlstm_gnorm[l]),
                        row(mlstm_skip[l]), batch=batch, seq=seq, tb=mix_tb, chunk=64)
        x2 = _mix_out(ya, yb, p, x2, w_pa_b, w_pb_b, w_out_b, row(g_post_mix[l]), mod3, l,
                      seq=seq, tile=tile)
        j = l // 2
        if l % 2 == 0:
            x2 = _ffn(x2, row(g_pre_ffn[l]), row(g_post_ffn[l]), mod3, *ffn_b, j,
                      seq=seq, tile=tile)
        else:
            rpad = jnp.pad(moe_router[j], ((0, 0), (0, LANES - N_EXPERTS)))
            x2 = _moe(x2, row(g_pre_ffn[l]), row(g_post_ffn[l]), mod3, rpad, *moe_b, j,
                      seq=seq, tile=tile, tf=1792)
    return x2.reshape(batch, seq, d)
```

```python
import functools

import jax
import jax.numpy as jnp
from jax import lax
from jax.experimental import pallas as pl
from jax.experimental.pallas import tpu as pltpu

F32 = jnp.float32
BF16 = jnp.bfloat16
EPS = 1e-6
NEG = -1e30
LANES = 128

D_MODEL = 1024
HGRN_HEADS = 8
HGRN_DH = 128
MLSTM_HEADS = 4
MLSTM_DH = 256
MLSTM_CONV = 4
QKV_BLOCK = 4
N_EXPERTS = 8
TOP_K = 2
N_SEG = 8
IN_TILE = 1024
IN_SUB_ROWS = 256
IN_SEGS_PER_STEP = 2
FFN_SUB_ROWS = 256

VMEM_LIMIT = 56 * 1024 * 1024


def _cparams(sem):
    return pltpu.CompilerParams(dimension_semantics=sem, vmem_limit_bytes=VMEM_LIMIT)


def _sigmoid(x):
    return 0.5 * jnp.tanh(0.5 * x) + 0.5


def _dot(a, b):
    return jnp.dot(a, b, preferred_element_type=F32)


def _dot_nt(a, b):
    return lax.dot_general(a, b, (((1,), (1,)), ((), ())), preferred_element_type=F32)


def _dot_tn(a, b):
    return lax.dot_general(a, b, (((0,), (0,)), ((), ())), preferred_element_type=F32)


def _split3(x):
    hi = x.astype(BF16)
    r1 = x - hi.astype(F32)
    mid = r1.astype(BF16)
    lo = (r1 - mid.astype(F32)).astype(BF16)
    return hi, mid, lo


def _chunk_tril(n, chunk):
    r = lax.broadcasted_iota(jnp.int32, (n, n), 0)
    c = lax.broadcasted_iota(jnp.int32, (n, n), 1)
    keep = (r >= c) & ((r // chunk) == (c // chunk))
    return jnp.where(keep, 1.0, 0.0).astype(BF16)


def _cumsum_rows(tri, x):
    hi, mid, lo = _split3(x)
    return _dot(tri, hi) + _dot(tri, mid) + _dot(tri, lo)


def _rms_mod(x, g, sc, sh):
    ms = jnp.mean(x * x, axis=-1, keepdims=True)
    return (x * lax.rsqrt(ms + EPS) * g) * (1.0 + sc) + sh


def _ada_kernel(c_ref, w_ref, b_ref, o_ref):
    c = c_ref[...]
    ca = (c * _sigmoid(c)).astype(BF16)
    o_ref[0] = _dot(ca, w_ref[0].astype(BF16)) + b_ref[0]


def _ada(c_pad, w_ada, b_ada):
    depth, d, six_d = w_ada.shape
    bp = c_pad.shape[0]
    tn = 1024
    return pl.pallas_call(
        _ada_kernel,
        grid=(depth, six_d // tn),
        in_specs=[
            pl.BlockSpec((bp, d), lambda l, j: (0, 0)),
            pl.BlockSpec((1, d, tn), lambda l, j: (l, 0, j)),
            pl.BlockSpec((1, 1, tn), lambda l, j: (l, 0, j)),
        ],
        out_specs=pl.BlockSpec((1, bp, tn), lambda l, j: (l, 0, j)),
        out_shape=jax.ShapeDtypeStruct((depth, bp, six_d), F32),
        compiler_params=_cparams(("parallel", "parallel")),
        name="ada_mod",
    )(c_pad, w_ada, b_ada.reshape(depth, 1, six_d))


def _in_kernel(x_ref, g_ref, sc_ref, sh_ref, lb_ref, w_ref, p_ref, lf_ref, h_scr):
    j = pl.program_id(1)

    @pl.when(j == 0)
    def _():
        h_scr[...] = _rms_mod(x_ref[...], g_ref[...], sc_ref[0], sh_ref[0]).astype(BF16)

    d = h_scr.shape[1]

    def run(*epilogues):
        for r0 in range(0, h_scr.shape[0], IN_SUB_ROWS):
            rows = slice(r0, r0 + IN_SUB_ROWS)
            for s, epilogue in enumerate(epilogues):
                cols = slice(s * d, (s + 1) * d)
                epilogue(rows, cols, _dot(h_scr[rows, :], w_ref[:, cols]))

    def silu(rows, cols, acc):
        p_ref[rows, cols] = (acc * _sigmoid(acc)).astype(BF16)

    def forget(rows, cols, acc):
        lb = lb_ref[...]
        sg = _sigmoid(acc)
        lf_ref[rows, :] = jnp.log(lb + (1.0 - lb) * sg)
        p_ref[rows, cols] = ((1.0 - lb) * (1.0 - sg)).astype(BF16)

    def ident(rows, cols, acc):
        p_ref[rows, cols] = acc.astype(BF16)

    def sigm(rows, cols, acc):
        p_ref[rows, cols] = _sigmoid(acc).astype(BF16)

    pl.when(j == 0)(lambda: run(silu, forget))
    pl.when(j == 1)(lambda: run(ident, silu))
    pl.when(j == 2)(lambda: run(ident, sigm))
    pl.when(j == 3)(lambda: run(sigm, sigm))


def _in_proj(x2, g_pre, mod3, lb, w_in, layer, *, seq, tile):
    n, d = x2.shape
    tiles_per_row = seq // tile
    return pl.pallas_call(
        _in_kernel,
        grid=(n // tile, N_SEG // IN_SEGS_PER_STEP),
        in_specs=[
            pl.BlockSpec((tile, d), lambda i, j: (i, 0)),
            pl.BlockSpec((1, d), lambda i, j: (0, 0)),
            pl.BlockSpec((1, 1, d), lambda i, j: (i // tiles_per_row, 0, 1)),
            pl.BlockSpec((1, 1, d), lambda i, j: (i // tiles_per_row, 0, 0)),
            pl.BlockSpec((1, d), lambda i, j: (0, 0)),
            pl.BlockSpec((None, d, IN_SEGS_PER_STEP * d), lambda i, j: (layer, 0, j)),
        ],
        out_specs=[
            pl.BlockSpec((tile, IN_SEGS_PER_STEP * d), lambda i, j: (i, j)),
            pl.BlockSpec((tile, d), lambda i, j: (i, 0)),
        ],
        out_shape=[
            jax.ShapeDtypeStruct((n, N_SEG * d), BF16),
            jax.ShapeDtypeStruct((n, d), F32),
        ],
        scratch_shapes=[pltpu.VMEM((tile, d), BF16)],
        compiler_params=_cparams(("parallel", "arbitrary")),
        name="in_proj",
    )(x2, g_pre, mod3, mod3, lb, w_in)


def _hgrn_units(q_ref, k_ref, v_ref, sg_ref, lf_ref, gn_ref, o_ref, st_scr, *, tb, chunk):
    tri = _chunk_tril(tb, chunk)
    b_all = _cumsum_rows(tri, lf_ref[0])
    r_io = lax.broadcasted_iota(jnp.int32, (chunk, chunk), 0)
    c_io = lax.broadcasted_iota(jnp.int32, (chunk, chunk), 1)
    causal = r_io >= c_io
    mid = chunk // 2 - 1

    def head_chunk(rs, h):
        cs = slice(h * HGRN_DH, (h + 1) * HGRN_DH)
        b = b_all[rs, cs]
        b_mid = b[mid:mid + 1]
        b_last = b[chunk - 1:chunk]
        v = v_ref[0, rs, cs]
        qi = q_ref[0, rs, cs].astype(F32) * jnp.exp(b - b_mid)
        ki = k_ref[0, rs, cs].astype(F32) * jnp.exp(b_mid - b)
        q_st = (qi * jnp.exp(b_mid)).astype(BF16)
        k_st = (ki * jnp.exp(b_last - b_mid)).astype(BF16)
        sc = jnp.where(causal, _dot_nt(qi.astype(BF16), ki.astype(BF16)), 0.0).astype(BF16)
        st = st_scr[h]
        o = _dot(sc, v) + _dot_nt(q_st, st.astype(BF16))
        st_scr[h] = st * jnp.exp(b_last) + _dot_tn(v, k_st)
        ms = jnp.mean(o * o, axis=-1, keepdims=True)
        y = o * lax.rsqrt(ms + EPS) * gn_ref[:, cs] * sg_ref[0, rs, cs].astype(F32)
        o_ref[0, rs, cs] = y.astype(BF16)

    def chunk_unit(ci):
        rs = slice(ci * chunk, (ci + 1) * chunk)
        for h in range(HGRN_HEADS):
            head_chunk(rs, h)

    return [functools.partial(chunk_unit, ci) for ci in range(tb // chunk)]


def _mlstm_units(xm_ref, so_ref, cw_ref, cb_ref, bdq_ref, bdk_ref, bdv_ref, wg_ref, bg_ref,
                 gn_ref, sk_ref, o_ref, xbuf, c_scr, n_scr, m_scr, *, L):
    d = D_MODEL
    xmb = xm_ref[0]
    xm = xmb.astype(F32)
    xbuf[8:8 + L, :] = xm
    cw = cw_ref[...]
    conv = (cb_ref[...] + cw[3:4] * xm + cw[2:3] * xbuf[7:7 + L, :]
            + cw[1:2] * xbuf[6:6 + L, :] + cw[0:1] * xbuf[5:5 + L, :])
    xbuf[0:8, :] = xm[L - 8:L]
    xc = conv * _sigmoid(conv)
    xcb = xc.astype(BF16)

    qs, ks, vs = [], [], []
    for g in range(d // LANES):
        gs = slice(g * LANES, (g + 1) * LANES)
        qs.append(_dot(xcb[:, gs], bdq_ref[g]))
        ks.append(_dot(xcb[:, gs], bdk_ref[g]))
        vs.append(_dot(xmb[:, gs], bdv_ref[g]))
    q = jnp.concatenate(qs, axis=-1)
    k = jnp.concatenate(ks, axis=-1)
    v = jnp.concatenate(vs, axis=-1)
    qb, kb, vb = q.astype(BF16), k.astype(BF16), v.astype(BF16)

    gates = (_dot(qb, wg_ref[0:d, :]) + _dot(kb, wg_ref[d:2 * d, :])
             + _dot(vb, wg_ref[2 * d:3 * d, :]) + bg_ref[...])
    lf = jnp.minimum(gates, 0.0) - jnp.log(1.0 + jnp.exp(-jnp.abs(gates)))
    bcs = _cumsum_rows(_chunk_tril(L, L), lf)
    gates_t = gates.T
    bcs_t = bcs.T

    r_io = lax.broadcasted_iota(jnp.int32, (L, L), 0)
    c_io = lax.broadcasted_iota(jnp.int32, (L, L), 1)
    causal = r_io >= c_io
    kscale = MLSTM_DH ** -0.5

    def head(h):
        cs = slice(h * MLSTM_DH, (h + 1) * MLSTM_DH)
        a_row = gates_t[h:h + 1, :] - bcs_t[4 + h:5 + h, :]
        b_col = bcs[:, 4 + h:5 + h]
        a_col = gates[:, h:h + 1] - b_col
        m_prev = m_scr[h][:, 0:1]
        am = jnp.where(causal, a_row, NEG)
        m_run = jnp.maximum(jnp.max(am, axis=-1, keepdims=True), m_prev)
        e = jnp.exp(am - m_run)
        qh = q[:, cs]
        khs = k[:, cs] * kscale
        s = _dot_nt(qb[:, cs], khs.astype(BF16)) * e
        w_inter = jnp.exp(m_prev - m_run)
        num = _dot(s.astype(BF16), vb[:, cs]) + w_inter * _dot(qb[:, cs], c_scr[h].astype(BF16))
        den = (jnp.sum(s, axis=-1, keepdims=True)
               + w_inter * jnp.sum(qh * n_scr[h], axis=-1, keepdims=True))
        m_t = b_col + m_run
        hh = num / jnp.maximum(jnp.abs(den), jnp.exp(-m_t))
        m_last = m_run[L - 1:L]
        w_s = jnp.exp(a_col - m_last)
        dec = jnp.exp(m_prev - m_last)
        kw = khs * w_s
        c_scr[h] = dec * c_scr[h] + _dot_tn(kw.astype(BF16), vb[:, cs])
        n_scr[h] = dec * n_scr[h] + jnp.sum(kw, axis=0, keepdims=True)
        m_scr[h] = jnp.broadcast_to(b_col[L - 1:L] + m_last, (1, LANES))
        mu = jnp.mean(hh, axis=-1, keepdims=True)
        hc = hh - mu
        var = jnp.mean(hc * hc, axis=-1, keepdims=True)
        y = hc * lax.rsqrt(var + EPS) * gn_ref[:, cs] + sk_ref[:, cs] * xc[:, cs]
        o_ref[0, :, cs] = (so_ref[0, :, cs].astype(F32) * y).astype(BF16)

    return [functools.partial(head, h) for h in range(MLSTM_HEADS)]


def _mixer_kernel(q_ref, k_ref, v_ref, sg_ref, lf_ref, gnh_ref, xm_ref, so_ref, cw_ref, cb_ref,
                  bdq_ref, bdk_ref, bdv_ref, wg_ref, bg_ref, gnm_ref, sk_ref, ya_ref, yb_ref,
                  st_scr, xbuf, c_scr, n_scr, m_scr, *, tb, chunk):
    @pl.when(pl.program_id(1) == 0)
    def _():
        st_scr[...] = jnp.zeros_like(st_scr)
        xbuf[0:8, :] = jnp.zeros((8, D_MODEL), F32)
        c_scr[...] = jnp.zeros_like(c_scr)
        n_scr[...] = jnp.zeros_like(n_scr)
        m_scr[...] = jnp.zeros_like(m_scr)

    m_units = _mlstm_units(xm_ref, so_ref, cw_ref, cb_ref, bdq_ref, bdk_ref, bdv_ref, wg_ref,
                           bg_ref, gnm_ref, sk_ref, yb_ref, xbuf, c_scr, n_scr, m_scr, L=tb)
    h_units = _hgrn_units(q_ref, k_ref, v_ref, sg_ref, lf_ref, gnh_ref, ya_ref, st_scr,
                          tb=tb, chunk=chunk)
    for i in range(max(len(m_units), len(h_units))):
        if i < len(h_units):
            h_units[i]()
        if i < len(m_units):
            m_units[i]()


def _mixer(p, lf, hgrn_gnorm, conv_w, conv_b, bdq, bdk, bdv, wg, bg, mlstm_gnorm, skip, *,
           batch, seq, tb, chunk):
    n, d = lf.shape
    p3 = p.reshape(batch, seq, p.shape[-1])
    col = lambda c: pl.BlockSpec((1, tb, d), lambda b, t: (b, t, c))
    full = lambda shp: pl.BlockSpec(shp, lambda b, t: (0,) * len(shp))
    vec = full((1, d))
    ya, yb = pl.pallas_call(
        functools.partial(_mixer_kernel, tb=tb, chunk=chunk),
        grid=(batch, seq // tb),
        in_specs=[col(0), col(1), col(2), col(3), col(0), vec,
                  col(4), col(5), full((MLSTM_CONV, d)), vec,
                  full(bdq.shape), full(bdk.shape), full(bdv.shape),
                  full(wg.shape), full((1, LANES)), vec, vec],
        out_specs=[col(0), col(0)],
        out_shape=[jax.ShapeDtypeStruct((batch, seq, d), BF16)] * 2,
        scratch_shapes=[
            pltpu.VMEM((HGRN_HEADS, HGRN_DH, HGRN_DH), F32),
            pltpu.VMEM((8 + tb, d), F32),
            pltpu.VMEM((MLSTM_HEADS, MLSTM_DH, MLSTM_DH), F32),
            pltpu.VMEM((MLSTM_HEADS, 1, MLSTM_DH), F32),
            pltpu.VMEM((MLSTM_HEADS, 1, LANES), F32),
        ],
        compiler_params=_cparams(("parallel", "arbitrary")),
        name="mixer",
    )(p3, p3, p3, p3, lf.reshape(batch, seq, d), hgrn_gnorm,
      p3, p3, conv_w, conv_b, bdq, bdk, bdv, wg, bg, mlstm_gnorm, skip)
    return ya.reshape(n, d), yb.reshape(n, d)


def _mix_out_kernel(ya_ref, yb_ref, ga_ref, gb_ref, x_ref, wa_ref, wb_ref, wo_ref, gp_ref, gt_ref,
                    o_ref):
    pa = _dot(ya_ref[...], wa_ref[...])
    pb = _dot(yb_ref[...], wb_ref[...])
    mixed = ga_ref[...].astype(F32) * pa + gb_ref[...].astype(F32) * pb
    y = _dot(mixed.astype(BF16), wo_ref[...])
    ms = jnp.mean(y * y, axis=-1, keepdims=True)
    o_ref[...] = x_ref[...] + gt_ref[0] * (y * lax.rsqrt(ms + EPS) * gp_ref[...])


def _mix_out(ya, yb, p, x2, wa, wb, wo, g_post, mod3, layer, *, seq, tile):
    n, d = x2.shape
    tpr = seq // tile
    row = lambda c: pl.BlockSpec((tile, d), lambda i: (i, c))
    wspec = pl.BlockSpec((None, d, d), lambda i: (layer, 0, 0))
    return pl.pallas_call(
        _mix_out_kernel,
        grid=(n // tile,),
        in_specs=[row(0), row(0), row(6), row(7), row(0), wspec, wspec, wspec,
                  pl.BlockSpec((1, d), lambda i: (0, 0)),
                  pl.BlockSpec((1, 1, d), lambda i: (i // tpr, 0, 2))],
        out_specs=row(0),
        out_shape=jax.ShapeDtypeStruct((n, d), F32),
        compiler_params=_cparams(("parallel",)),
        name="mix_out",
    )(ya, yb, p, p, x2, wa, wb, wo, g_post, mod3)


def _ffn_kernel(x_ref, g_ref, sc_ref, sh_ref, gp_ref, gt_ref, w1_ref, w3_ref, w2_ref, o_ref):
    for r0 in range(0, x_ref.shape[0], FFN_SUB_ROWS):
        rows = slice(r0, r0 + FFN_SUB_ROWS)
        x = x_ref[rows, :]
        h = _rms_mod(x, g_ref[...], sc_ref[0], sh_ref[0]).astype(BF16)
        a = _dot(h, w1_ref[...])
        act = (a * _sigmoid(a)) * _dot(h, w3_ref[...])
        y = _dot(act.astype(BF16), w2_ref[...])
        ms = jnp.mean(y * y, axis=-1, keepdims=True)
        o_ref[rows, :] = x + gt_ref[0] * (y * lax.rsqrt(ms + EPS) * gp_ref[...])


def _ffn(x2, g_pre, g_post, mod3, w1, w3, w2, layer, *, seq, tile):
    n, d = x2.shape
    dff = w1.shape[-1]
    tpr = seq // tile
    vec = pl.BlockSpec((1, d), lambda i: (0, 0))
    modspec = lambda c: pl.BlockSpec((1, 1, d), lambda i: (i // tpr, 0, c))
    resident = dict(pipeline_mode=pl.Buffered(1))
    return pl.pallas_call(
        _ffn_kernel,
        grid=(n // tile,),
        in_specs=[pl.BlockSpec((tile, d), lambda i: (i, 0)), vec, modspec(4), modspec(3), vec,
                  modspec(5),
                  pl.BlockSpec((None, d, dff), lambda i: (layer, 0, 0), **resident),
                  pl.BlockSpec((None, d, dff), lambda i: (layer, 0, 0), **resident),
                  pl.BlockSpec((None, dff, d), lambda i: (layer, 0, 0), **resident)],
        out_specs=pl.BlockSpec((tile, d), lambda i: (i, 0)),
        out_shape=jax.ShapeDtypeStruct((n, d), F32),
        compiler_params=_cparams(("parallel",)),
        name="ffn",
    )(x2, g_pre, mod3, mod3, g_post, mod3, w1, w3, w2)


SLOT_TILE = 512
SEG_ALIGN = 16


def _route_kernel(x_ref, g_ref, sc_ref, sh_ref, rt_ref, h_ref, meta_ref, cnt_ref):
    tile = x_ref.shape[0]
    lane = lax.broadcasted_iota(jnp.int32, (tile, LANES), 1)
    h = _rms_mod(x_ref[...], g_ref[...], sc_ref[0], sh_ref[0])
    h_hi = h.astype(BF16)
    h_ref[...] = h_hi
    h_lo = (h - h_hi.astype(F32)).astype(BF16)
    r = rt_ref[...]
    r_hi = r.astype(BF16)
    r_lo = (r - r_hi.astype(F32)).astype(BF16)
    logits = _dot(h_hi, r_hi) + _dot(h_hi, r_lo) + _dot(h_lo, r_hi)
    logits = jnp.where(lane < N_EXPERTS, logits, NEG)
    lane_f = lane.astype(F32)
    v1 = jnp.max(logits, axis=-1, keepdims=True)
    i1 = jnp.min(jnp.where(logits == v1, lane_f, float(LANES)), axis=-1, keepdims=True)
    m1 = lane_f == i1
    l2 = jnp.where(m1, NEG, logits)
    v2 = jnp.max(l2, axis=-1, keepdims=True)
    i2 = jnp.min(jnp.where(l2 == v2, lane_f, float(LANES)), axis=-1, keepdims=True)
    m2 = lane_f == i2
    e2 = jnp.exp(v2 - v1)
    inv = 1.0 / (1.0 + e2)
    onehot = jnp.where(m1, 1.0, jnp.where(m2, 1.0, 0.0))
    r_io = lax.broadcasted_iota(jnp.int32, (tile, tile), 0)
    c_io = lax.broadcasted_iota(jnp.int32, (tile, tile), 1)
    before = jnp.where(r_io > c_io, 1.0, 0.0).astype(BF16)
    rank = _dot(before, onehot.astype(BF16))
    r1 = jnp.sum(jnp.where(m1, rank, 0.0), axis=-1, keepdims=True)
    r2 = jnp.sum(jnp.where(m2, rank, 0.0), axis=-1, keepdims=True)
    cols = (i1, i2, r1, r2, inv, e2 * inv)
    meta = jnp.zeros((tile, LANES), F32)
    for idx, val in enumerate(cols):
        meta = jnp.where(lane == idx, val, meta)
    meta_ref[...] = meta
    cnt_ref[0] = jnp.sum(onehot, axis=0, keepdims=True)


def _route(x2, g_pre, mod3, router_pad, *, seq, tile):
    n, d = x2.shape
    tpr = seq // tile
    nt = n // tile
    vec = pl.BlockSpec((1, d), lambda i: (0, 0))
    modspec = lambda c: pl.BlockSpec((1, 1, d), lambda i: (i // tpr, 0, c))
    return pl.pallas_call(
        _route_kernel,
        grid=(nt,),
        in_specs=[pl.BlockSpec((tile, d), lambda i: (i, 0)), vec, modspec(4), modspec(3),
                  pl.BlockSpec((d, LANES), lambda i: (0, 0))],
        out_specs=[pl.BlockSpec((tile, d), lambda i: (i, 0)),
                   pl.BlockSpec((tile, LANES), lambda i: (i, 0)),
                   pl.BlockSpec((1, 1, LANES), lambda i: (i, 0, 0))],
        out_shape=[jax.ShapeDtypeStruct((n, d), BF16),
                   jax.ShapeDtypeStruct((n, LANES), F32),
                   jax.ShapeDtypeStruct((nt, 1, LANES), F32)],
        compiler_params=_cparams(("parallel",)),
        name="moe_route",
    )(x2, g_pre, mod3, mod3, router_pad)


def _segment_dma(start_ref, len_ref, i, hbm_ref, buf_ref, sem, *, to_hbm, wait):
    loc = jnp.int32(0)
    for e in range(N_EXPERTS):
        base = start_ref[i * N_EXPERTS + e]
        length = len_ref[i * N_EXPERTS + e]

        def chunk(c, carry, base=base, loc=loc):
            hb = hbm_ref.at[pl.ds(pl.multiple_of(base + c * SEG_ALIGN, SEG_ALIGN), SEG_ALIGN)]
            vb = buf_ref.at[pl.ds(pl.multiple_of(loc + c * SEG_ALIGN, SEG_ALIGN), SEG_ALIGN)]
            cp = pltpu.make_async_copy(vb, hb, sem) if to_hbm else pltpu.make_async_copy(hb, vb, sem)
            if wait:
                cp.wait()
            else:
                cp.start()
            return carry

        lax.fori_loop(0, length // SEG_ALIGN, chunk, 0)
        loc = loc + length


def _dispatch_kernel(start_ref, len_ref, p1_ref, p2_ref, h_ref, init_ref, hs_ref, buf, sem):
    del init_ref
    i = pl.program_id(0)
    rows = buf.shape[0]
    tk = h_ref.shape[0]
    slot = lax.broadcasted_iota(jnp.int32, (rows, tk), 0)
    sel = jnp.where(slot == p1_ref[0], 1.0, jnp.where(slot == p2_ref[0], 1.0, 0.0)).astype(BF16)
    buf[...] = _dot(sel, h_ref[...]).astype(BF16)
    _segment_dma(start_ref, len_ref, i, hs_ref, buf, sem, to_hbm=True, wait=False)
    _segment_dma(start_ref, len_ref, i, hs_ref, buf, sem, to_hbm=True, wait=True)


def _dispatch(seg_start, seg_len, lpos1_rows, lpos2_rows, h, n_slots, local_rows):
    n, d = h.shape
    tk = lpos1_rows.shape[-1]
    hs_init = jnp.zeros((n_slots, d), BF16)
    grid_spec = pltpu.PrefetchScalarGridSpec(
        num_scalar_prefetch=2,
        grid=(n // tk,),
        in_specs=[pl.BlockSpec((1, 1, tk), lambda i, st, ln: (i, 0, 0)),
                  pl.BlockSpec((1, 1, tk), lambda i, st, ln: (i, 0, 0)),
                  pl.BlockSpec((tk, d), lambda i, st, ln: (i, 0)),
                  pl.BlockSpec(memory_space=pl.ANY)],
        out_specs=pl.BlockSpec(memory_space=pl.ANY),
        scratch_shapes=[pltpu.VMEM((local_rows, d), BF16), pltpu.SemaphoreType.DMA(())],
    )
    return pl.pallas_call(
        _dispatch_kernel,
        grid_spec=grid_spec,
        out_shape=jax.ShapeDtypeStruct((n_slots, d), BF16),
        input_output_aliases={5: 0},
        compiler_params=_cparams(("arbitrary",)),
        name="moe_dispatch",
    )(seg_start, seg_len, lpos1_rows, lpos2_rows, h, hs_init)


def _group_kernel(te_ref, nu_ref, hs_ref, w1_ref, w3_ref, w2_ref, y_ref, acc_scr):
    m = pl.program_id(0)
    j = pl.program_id(1)
    last = pl.num_programs(1) - 1
    used = m < nu_ref[0]

    @pl.when(used)
    def _():
        @pl.when(j == 0)
        def _():
            acc_scr[...] = jnp.zeros_like(acc_scr)

        for r0 in range(0, hs_ref.shape[0], FFN_SUB_ROWS):
            rows = slice(r0, r0 + FFN_SUB_ROWS)
            hs = hs_ref[rows, :]
            a = _dot(hs, w1_ref[0])
            act = (a * _sigmoid(a)) * _dot(hs, w3_ref[0])
            acc_scr[rows, :] += _dot(act.astype(BF16), w2_ref[0])

        @pl.when(j == last)
        def _():
            y_ref[...] = acc_scr[...].astype(BF16)

    @pl.when(jnp.logical_not(used) & (j == last))
    def _():
        y_ref[...] = jnp.zeros_like(y_ref)


def _group(tile_expert, n_used, hs, w1, w3, w2, layer, *, tf):
    ns, d = hs.shape
    dff = w1.shape[-1]
    nj = dff // tf

    def wj(m, j, nu):
        return jnp.where(m < nu[0], j, nj - 1)

    grid_spec = pltpu.PrefetchScalarGridSpec(
        num_scalar_prefetch=2,
        grid=(ns // SLOT_TILE, nj),
        in_specs=[pl.BlockSpec((SLOT_TILE, d), lambda m, j, te, nu: (m, 0)),
                  pl.BlockSpec((None, 1, d, tf),
                               lambda m, j, te, nu: (layer, te[m], 0, wj(m, j, nu))),
                  pl.BlockSpec((None, 1, d, tf),
                               lambda m, j, te, nu: (layer, te[m], 0, wj(m, j, nu))),
                  pl.BlockSpec((None, 1, tf, d),
                               lambda m, j, te, nu: (layer, te[m], wj(m, j, nu), 0))],
        out_specs=pl.BlockSpec((SLOT_TILE, d), lambda m, j, te, nu: (m, 0)),
        scratch_shapes=[pltpu.VMEM((SLOT_TILE, d), F32)],
    )
    return pl.pallas_call(
        _group_kernel,
        grid_spec=grid_spec,
        out_shape=jax.ShapeDtypeStruct((ns, d), BF16),
        compiler_params=_cparams(("arbitrary", "arbitrary")),
        name="moe_group",
    )(tile_expert, n_used, hs, w1, w3, w2)


def _combine_kernel(start_ref, len_ref, meta_ref, y_ref, x_ref, gp_ref, gt_ref, o_ref, ybuf, sem):
    i = pl.program_id(0)
    tile = meta_ref.shape[0]
    rows = ybuf.shape[0]

    @pl.when(i == 0)
    def _():
        ybuf[...] = jnp.zeros_like(ybuf)

    _segment_dma(start_ref, len_ref, i, y_ref, ybuf, sem, to_hbm=False, wait=False)

    lane1 = lax.broadcasted_iota(jnp.int32, (1, LANES), 1)
    loc_row = jnp.zeros((1, LANES), F32)
    loc = jnp.int32(0)
    for e in range(N_EXPERTS):
        loc_row = jnp.where(lane1 == e, loc.astype(F32), loc_row)
        loc = loc + len_ref[i * N_EXPERTS + e]
    meta = meta_ref[...]
    lane_f = lax.broadcasted_iota(jnp.int32, (tile, LANES), 1).astype(F32)
    pos1 = meta[:, 2:3] + jnp.sum(jnp.where(lane_f == meta[:, 0:1], loc_row, 0.0), axis=-1,
                                  keepdims=True)
    pos2 = meta[:, 3:4] + jnp.sum(jnp.where(lane_f == meta[:, 1:2], loc_row, 0.0), axis=-1,
                                  keepdims=True)
    slot = lax.broadcasted_iota(jnp.int32, (tile, rows), 1).astype(F32)
    g = jnp.where(slot == pos1, meta[:, 4:5], jnp.where(slot == pos2, meta[:, 5:6], 0.0))

    _segment_dma(start_ref, len_ref, i, y_ref, ybuf, sem, to_hbm=False, wait=True)
    y = _dot(g.astype(BF16), ybuf[...])
    ms = jnp.mean(y * y, axis=-1, keepdims=True)
    o_ref[...] = x_ref[...] + gt_ref[0] * (y * lax.rsqrt(ms + EPS) * gp_ref[...])


def _combine(seg_start, seg_len, meta, y, x2, g_post, mod3, local_rows, *, seq, tile):
    n, d = x2.shape
    tpr = seq // tile
    grid_spec = pltpu.PrefetchScalarGridSpec(
        num_scalar_prefetch=2,
        grid=(n // tile,),
        in_specs=[pl.BlockSpec((tile, LANES), lambda i, st, ln: (i, 0)),
                  pl.BlockSpec(memory_space=pl.ANY),
                  pl.BlockSpec((tile, d), lambda i, st, ln: (i, 0)),
                  pl.BlockSpec((1, d), lambda i, st, ln: (0, 0)),
                  pl.BlockSpec((1, 1, d), lambda i, st, ln: (i // tpr, 0, 5))],
        out_specs=pl.BlockSpec((tile, d), lambda i, st, ln: (i, 0)),
        scratch_shapes=[pltpu.VMEM((local_rows, d), BF16), pltpu.SemaphoreType.DMA(())],
    )
    return pl.pallas_call(
        _combine_kernel,
        grid_spec=grid_spec,
        out_shape=jax.ShapeDtypeStruct((n, d), F32),
        compiler_params=_cparams(("arbitrary",)),
        name="moe_combine",
    )(seg_start, seg_len, meta, y, x2, g_post, mod3)


def _round_up(v, m):
    return -(-v // m) * m


def _moe(x2, g_pre, g_post, mod3, router_pad, w1, w3, w2, layer, *, seq, tile, tf):
    n, d = x2.shape
    nt = n // tile
    i32 = jnp.int32
    h, meta, cnt = _route(x2, g_pre, mod3, router_pad, seq=seq, tile=tile)

    cnt = cnt[:, 0, :N_EXPERTS].astype(i32)
    seg_len = _round_up(cnt, SEG_ALIGN)
    seg_cum = jnp.cumsum(seg_len, axis=0)
    padded = _round_up(seg_cum[-1], SLOT_TILE)
    region_end = jnp.cumsum(padded)
    seg_start = (region_end - padded)[None, :] + seg_cum - seg_len
    loc = jnp.cumsum(seg_len, axis=1) - seg_len
    local_rows = _round_up(TOP_K * tile + N_EXPERTS * (SEG_ALIGN - 1), LANES)
    n_slots = _round_up(TOP_K * n + N_EXPERTS * (nt * (SEG_ALIGN - 1) + SLOT_TILE - 1), SLOT_TILE)
    n_slot_tiles = n_slots // SLOT_TILE
    n_used = (region_end[-1] // SLOT_TILE).astype(i32)
    tile_first = jnp.minimum(jnp.arange(n_slot_tiles, dtype=i32), n_used - 1) * SLOT_TILE
    tile_expert = jnp.sum((region_end[None, :] <= tile_first[:, None]).astype(i32), axis=1)

    def local_pos(expert, rank):
        expert = expert.astype(i32).reshape(nt, tile)
        base = sum(jnp.where(expert == e, loc[:, e:e + 1], 0) for e in range(N_EXPERTS))
        return base + rank.astype(i32).reshape(nt, tile)

    lpos1 = local_pos(meta[:, 0], meta[:, 2])
    lpos2 = local_pos(meta[:, 1], meta[:, 3])

    seg_start = seg_start.reshape(-1)
    seg_len = seg_len.reshape(-1)
    hs = _dispatch(seg_start, seg_len, lpos1.reshape(nt, 1, tile), lpos2.reshape(nt, 1, tile), h,
                   n_slots, local_rows)
    y = _group(tile_expert, n_used.reshape(1), hs, w1, w3, w2, layer, tf=tf)
    return _combine(seg_start, seg_len, meta, y, x2, g_post, mod3, local_rows, seq=seq, tile=tile)


def _blockdiag_tiles(w):
    nb = w.shape[0]
    per = LANES // QKV_BLOCK
    wt = w.reshape(nb // per, per, QKV_BLOCK, QKV_BLOCK)
    eye = jnp.eye(per, dtype=w.dtype)
    t = jnp.einsum("gnio,nm->gnimo", wt, eye)
    return t.reshape(nb // per, LANES, LANES).astype(BF16)


def _lower_bounds(lb_raw):
    p = jax.nn.softmax(lb_raw.astype(F32), axis=0)
    return jnp.cumsum(p, axis=0) - p[0:1]


def kernel(x, c, w_ada, b_ada, g_pre_mix, g_post_mix, g_pre_ffn, g_post_ffn, w_in, hgrn_lb, hgrn_gnorm, mlstm_conv_w, mlstm_conv_b, mlstm_wq, mlstm_wk, mlstm_wv, mlstm_w_ig, mlstm_b_ig, mlstm_w_fg, mlstm_b_fg, mlstm_gnorm, mlstm_skip, w_proj_a, w_proj_b, w_out, ffn_w1, ffn_w3, ffn_w2, moe_router, moe_w1, moe_w3, moe_w2):
    batch, seq, d = x.shape
    depth = w_in.shape[0]
    n = batch * seq
    tile = min(512, seq)
    mix_tb = min(256, seq)

    bp = -(-batch // 16) * 16
    c_pad = jnp.pad(c, ((0, bp - batch), (0, 0)))
    mod = _ada(c_pad, w_ada, b_ada)
    lower = _lower_bounds(hgrn_lb)

    w_in_b = w_in.astype(BF16)
    w_pa_b, w_pb_b, w_out_b = (w.astype(BF16) for w in (w_proj_a, w_proj_b, w_out))
    ffn_b = tuple(w.astype(BF16) for w in (ffn_w1, ffn_w3, ffn_w2))
    moe_b = tuple(w.astype(BF16) for w in (moe_w1, moe_w3, moe_w2))

    x2 = x.reshape(n, d)
    row = lambda a: a.reshape(1, -1)
    for l in range(depth):
        mod3 = mod[l].reshape(bp, 1, 6 * d)
        p, lf = _in_proj(x2, row(g_pre_mix[l]), mod3, row(lower[l]), w_in_b, l,
                         seq=seq, tile=min(IN_TILE, seq))
        wg = jnp.concatenate([mlstm_w_ig[l], mlstm_w_fg[l]], axis=-1)
        wg = jnp.pad(wg, ((0, 0), (0, LANES - 2 * MLSTM_HEADS))).astype(BF16)
        bg = jnp.pad(jnp.concatenate([mlstm_b_ig[l], mlstm_b_fg[l]]), (0, LANES - 2 * MLSTM_HEADS))
        ya, yb = _mixer(p, lf, row(hgrn_gnorm[l]), mlstm_conv_w[l], row(mlstm_conv_b[l]),
                        _blockdiag_tiles(mlstm_wq[l]), _blockdiag_tiles(mlstm_wk[l]),
                        _blockdiag_tiles(mlstm_wv[l]), wg, row(bg), row(mlstm_gnorm[l]),
                        row(mlstm_skip[l]), batch=batch, seq=seq, tb=mix_tb, chunk=64)
        x2 = _mix_out(ya, yb, p, x2, w_pa_b, w_pb_b, w_out_b, row(g_post_mix[l]), mod3, l,
                      seq=seq, tile=tile)
        j = l // 2
        if l % 2 == 0:
            x2 = _ffn(x2, row(g_pre_ffn[l]), row(g_post_ffn[l]), mod3, *ffn_b, j,
                      seq=seq, tile=tile)
        else:
            rpad = jnp.pad(moe_router[j], ((0, 0), (0, LANES - N_EXPERTS)))
            x2 = _moe(x2, row(g_pre_ffn[l]), row(g_post_ffn[l]), mod3, rpad, *moe_b, j,
                      seq=seq, tile=tile, tf=1792)
    return x2.reshape(batch, seq, d)
```
